```python
import math
import jax, jax.numpy as jnp
from jax import lax
import numpy as np


D_MODEL = 1024
BATCH = 8
SEQ = 4096
DEPTH = 1

HEAD_DIM = 64
N_HEADS_A = 8
N_KV_A = 2
N_HEADS_B = 8
N_KV_B = 2
N_HEADS = N_HEADS_A + N_HEADS_B
MIX_WIDTH = N_HEADS * HEAD_DIM
WINDOW_A = 128
CMP_LEN = 32
CMP_STRIDE = 16
CMP_HIDDEN = 128
SEL_LEN = 64
SEL_TOPK = 16
WINDOW_B = 512
FORCE_BONUS = 1000.0
BAND_BLOCK = 128
NSA_BLOCK = 32
NUM_BUCKETS = 32
MAX_DISTANCE = 128
D_FF = 2816
EPS = 1e-6
NEG = -1e9

QA = N_HEADS_A * HEAD_DIM
KVA = N_KV_A * HEAD_DIM
QB = N_HEADS_B * HEAD_DIM
KVB = N_KV_B * HEAD_DIM
N_GATES = 3 * N_HEADS_B
OFF_KA = QA
OFF_VA = OFF_KA + KVA
OFF_QB = OFF_VA + KVA
OFF_KVB = OFF_QB + QB
OFF_GB = OFF_KVB + 6 * KVB
D_IN = OFF_GB + N_GATES

kernel_name = 'hymba_swa_sink_nsa_macaron_block'


def rms_norm(x, g):
    x32 = x.astype(jnp.float32)
    y = x32 * lax.rsqrt(jnp.mean(x32 * x32, axis=-1, keepdims=True) + EPS) * g.astype(jnp.float32)
    return y.astype(x.dtype)


def swiglu(h, w_in, w_out):
    gate, up = jnp.split(h @ w_in, 2, axis=-1)
    return (jax.nn.silu(gate) * up) @ w_out


def rel_bucket(rel):
    n = jnp.maximum(rel, 0)
    max_exact = NUM_BUCKETS // 2
    nf = jnp.maximum(n, 1).astype(jnp.float32)
    large = max_exact + (jnp.log(nf / max_exact) / math.log(MAX_DISTANCE / max_exact)
                         * (NUM_BUCKETS - max_exact)).astype(jnp.int32)
    large = jnp.minimum(large, NUM_BUCKETS - 1)
    return jnp.where(n < max_exact, n, large)


def rel_bias(tbl, rel):
    return tbl[rel_bucket(rel)].astype(jnp.float32)


def banded_attention(q, k, v, tbl, window, sinks):
    B, S, G, R, Dh = q.shape
    kv_len = window + BAND_BLOCK
    k_pad = jnp.pad(k, ((0, 0), (window, 0), (0, 0), (0, 0)))
    v_pad = jnp.pad(v, ((0, 0), (window, 0), (0, 0), (0, 0)))

    def block(i):
        start = i * BAND_BLOCK
        qb = lax.dynamic_slice_in_dim(q, start, BAND_BLOCK, axis=1)
        kb = lax.dynamic_slice_in_dim(k_pad, start, kv_len, axis=1)
        vb = lax.dynamic_slice_in_dim(v_pad, start, kv_len, axis=1)
        q_pos = start + jnp.arange(BAND_BLOCK)
        k_pos = start - window + jnp.arange(kv_len)
        rel = q_pos[:, None] - k_pos[None, :]
        mask = (rel >= 0) & (rel < window) & (k_pos[None, :] >= 0)
        bias = rel_bias(tbl, rel).reshape(BAND_BLOCK, kv_len, G, R).transpose(2, 3, 0, 1)
        logits = jnp.einsum('bqgrd,bkgd->bgrqk', qb, kb).astype(jnp.float32) + bias
        logits = jnp.where(mask, logits, NEG)
        m = logits.max(axis=-1, keepdims=True)
        if sinks is None:
            p = jnp.exp(logits - m)
            denom = p.sum(axis=-1, keepdims=True)
        else:
            s = sinks.astype(jnp.float32).reshape(1, G, R, 1, 1)
            m = jnp.maximum(m, s)
            p = jnp.exp(logits - m)
            denom = p.sum(axis=-1, keepdims=True) + jnp.exp(s - m)
        p = p / denom
        return jnp.einsum('bgrqk,bkgd->bqgrd', p.astype(vb.dtype), vb)

    out = lax.map(block, jnp.arange(S // BAND_BLOCK))
    return out.transpose(1, 0, 2, 3, 4, 5).reshape(B, S, G, R, Dh)


def compress_kv(kv, pos, w1, b1, w2, b2):
    _, B, S, G, Dh = kv.shape
    r = CMP_LEN // CMP_STRIDE
    n_chunk = S // CMP_STRIDE
    n_cmp = n_chunk - r + 1
    chunks = kv.reshape(2, B, n_chunk, CMP_STRIDE, G, Dh)
    w1r = w1.reshape(2, r, CMP_STRIDE, Dh, CMP_HIDDEN)
    pre = b1[:, None, None, None, :]
    for j in range(r):
        pe = pos[:, None, None, j * CMP_STRIDE:(j + 1) * CMP_STRIDE, None, :]
        part = jnp.einsum('kbnpgd,kpdh->kbngh', chunks + pe, w1r[:, j])
        pre = pre + part[:, :, j:j + n_cmp]
    hid = jax.nn.silu(pre)
    return jnp.einsum('kbcgh,khd->kbcgd', hid, w2) + b2[:, None, None, None, :]


def nsa_cmp_sel(q, k_c, v_c, k_s, v_s, tbl):
    B, S, G, R, Dh = q.shape
    n_cmp = k_c.shape[1]
    n_slc = S // SEL_LEN
    n_top = min(SEL_TOPK, n_slc)
    cmp_start = jnp.arange(n_cmp) * CMP_STRIDE
    cmp_end = cmp_start + CMP_LEN - 1
    slc_start = jnp.arange(n_slc) * SEL_LEN
    overlap = ((cmp_start[:, None] < slc_start[None, :] + SEL_LEN)
               & (cmp_end[:, None] >= slc_start[None, :])).astype(jnp.float32)
    kb = k_s.reshape(B, n_slc, SEL_LEN, G, Dh).transpose(0, 3, 1, 2, 4)
    vb = v_s.reshape(B, n_slc, SEL_LEN, G, Dh).transpose(0, 3, 1, 2, 4)
    tbl3 = tbl.reshape(NUM_BUCKETS, G, R)
    b_idx = jnp.arange(B)[:, None, None, None]
    g_idx = jnp.arange(G)[None, :, None, None]
    blk = jnp.arange(n_slc)
    offs = jnp.arange(SEL_LEN)

    def block(i):
        start = i * NSA_BLOCK
        qb = lax.dynamic_slice_in_dim(q, start, NSA_BLOCK, axis=1)
        q_pos = start + jnp.arange(NSA_BLOCK)
        rel_c = q_pos[:, None] - cmp_end[None, :]
        mask_c = rel_c >= 0
        bias_c = rel_bias(tbl, rel_c).reshape(NSA_BLOCK, n_cmp, G, R).transpose(2, 3, 0, 1)
        lc = jnp.einsum('bqgrd,bcgd->bgrqc', qb, k_c).astype(jnp.float32) + bias_c
        lc = jnp.where(mask_c, lc, NEG)
        pc = jnp.where(mask_c, jnp.exp(lc - lc.max(axis=-1, keepdims=True)), 0.0)
        pc = pc / jnp.maximum(pc.sum(axis=-1, keepdims=True), jnp.finfo(jnp.float32).tiny)
        o_cmp = jnp.einsum('bgrqc,bcgd->bqgrd', pc.astype(v_c.dtype), v_c)
        imp = jnp.einsum('bgrqc,cn->bgqn', pc, overlap)
        cur = q_pos[:, None] // SEL_LEN
        valid = blk[None, :] <= cur
        forced = (blk[None, :] == 0) | (blk[None, :] == cur) | (blk[None, :] == cur - 1)
        score = jnp.where(valid, imp + jnp.where(forced, FORCE_BONUS, 0.0), NEG)
        top_val, top_idx = lax.top_k(score, n_top)
        sel_ok = top_val > 0.5 * NEG
        ks = kb[b_idx, g_idx, top_idx]
        vs = vb[b_idx, g_idx, top_idx]
        tok = top_idx[..., None] * SEL_LEN + offs
        rel_s = q_pos[None, None, :, None, None] - tok
        mask_s = (sel_ok[..., None] & (rel_s >= 0))[:, :, None]
        bias_s = jnp.moveaxis(tbl3[rel_bucket(rel_s), g_idx[..., None]], -1, 2).astype(jnp.float32)
        ls = jnp.einsum('bqgrd,bgqnld->bgrqnl', qb, ks).astype(jnp.float32) + bias_s
        ls = jnp.where(mask_s, ls, NEG)
        ps = jax.nn.softmax(ls.reshape(B, G, R, NSA_BLOCK, n_top * SEL_LEN), axis=-1).reshape(ls.shape)
        o_sel = jnp.einsum('bgrqnl,bgqnld->bqgrd', ps.astype(vs.dtype), vs)
        return o_cmp, o_sel

    o_cmp, o_sel = lax.map(block, jnp.arange(S // NSA_BLOCK))
    o_cmp = o_cmp.transpose(1, 0, 2, 3, 4, 5).reshape(B, S, G, R, Dh)
    o_sel = o_sel.transpose(1, 0, 2, 3, 4, 5).reshape(B, S, G, R, Dh)
    return o_cmp, o_sel


def hybrid_layer(x, tbl, ffn1_norm, ffn1_w_in, ffn1_w_out, mix_norm, w_mix_in, w_mix_out,
                 q_norm_a, k_norm_a, sinks_a, q_norm_b, k_norm_b,
                 cmp_pos, cmp_w1, cmp_b1, cmp_w2, cmp_b2,
                 ffn2_norm, ffn2_w_in, ffn2_w_out):
    B, S, _ = x.shape
    r_a = N_HEADS_A // N_KV_A
    r_b = N_HEADS_B // N_KV_B
    scale = HEAD_DIM ** -0.5
    x = x + 0.5 * swiglu(rms_norm(x, ffn1_norm), ffn1_w_in, ffn1_w_out)
    h = rms_norm(x, mix_norm)
    proj = h @ w_mix_in
    q_a = rms_norm(proj[..., :OFF_KA].reshape(B, S, N_KV_A, r_a, HEAD_DIM), q_norm_a) * scale
    k_a = rms_norm(proj[..., OFF_KA:OFF_VA].reshape(B, S, N_KV_A, HEAD_DIM), k_norm_a)
    v_a = proj[..., OFF_VA:OFF_QB].reshape(B, S, N_KV_A, HEAD_DIM)
    o_a = banded_attention(q_a, k_a, v_a, tbl[:, :N_HEADS_A], WINDOW_A, sinks_a)
    q_b = rms_norm(proj[..., OFF_QB:OFF_KVB].reshape(B, S, N_KV_B, r_b, HEAD_DIM), q_norm_b) * scale
    kv_b = proj[..., OFF_KVB:OFF_GB].reshape(B, S, 6, N_KV_B, HEAD_DIM)
    kv_c = compress_kv(jnp.moveaxis(kv_b[:, :, 0:2], 2, 0), cmp_pos, cmp_w1, cmp_b1, cmp_w2, cmp_b2)
    k_c = rms_norm(kv_c[0], k_norm_b)
    v_c = kv_c[1]
    k_s = rms_norm(kv_b[:, :, 2], k_norm_b)
    v_s = kv_b[:, :, 3]
    k_w = rms_norm(kv_b[:, :, 4], k_norm_b)
    v_w = kv_b[:, :, 5]
    tbl_b = tbl[:, N_HEADS_A:]
    o_cmp, o_sel = nsa_cmp_sel(q_b, k_c, v_c, k_s, v_s, tbl_b)
    o_win = banded_attention(q_b, k_w, v_w, tbl_b, WINDOW_B, None)
    gates = jax.nn.sigmoid(proj[..., OFF_GB:].astype(jnp.float32)).reshape(B, S, 3, N_KV_B, r_b, 1)
    o_b = gates[:, :, 0] * o_cmp + gates[:, :, 1] * o_sel + gates[:, :, 2] * o_win
    mix = jnp.concatenate([o_a.reshape(B, S, QA).astype(x.dtype),
                           o_b.reshape(B, S, QB).astype(x.dtype)], axis=-1)
    x = x + mix @ w_mix_out
    x = x + 0.5 * swiglu(rms_norm(x, ffn2_norm), ffn2_w_in, ffn2_w_out)
    return x


def setup_inputs(seed: int = 0) -> dict:
    key = jax.random.key(seed)
    ks = jax.random.split(key, 24)
    f32 = jnp.float32

    def nrm(k, shape, scale):
        return jax.random.normal(k, shape, f32) * scale

    def gain(k, shape):
        return 1.0 + 0.05 * jax.random.normal(k, shape, f32)

    L = DEPTH
    return {
        'x': nrm(ks[0], (BATCH, SEQ, D_MODEL), 1.0),
        'rel_bias_table': nrm(ks[1], (NUM_BUCKETS, N_HEADS), 0.5),
        'ffn1_norm': gain(ks[2], (L, D_MODEL)),
        'ffn1_w_in': nrm(ks[3], (L, D_MODEL, 2 * D_FF), D_MODEL ** -0.5),
        'ffn1_w_out': nrm(ks[4], (L, D_FF, D_MODEL), D_FF ** -0.5),
        'mix_norm': gain(ks[5], (L, D_MODEL)),
        'w_mix_in': nrm(ks[6], (L, D_MODEL, D_IN), D_MODEL ** -0.5),
        'w_mix_out': nrm(ks[7], (L, MIX_WIDTH, D_MODEL), MIX_WIDTH ** -0.5),
        'q_norm_a': gain(ks[8], (L, HEAD_DIM)),
        'k_norm_a': gain(ks[9], (L, HEAD_DIM)),
        'sinks_a': nrm(ks[10], (L, N_HEADS_A), 0.5),
        'q_norm_b': gain(ks[11], (L, HEAD_DIM)),
        'k_norm_b': gain(ks[12], (L, HEAD_DIM)),
        'cmp_pos': nrm(ks[13], (L, 2, CMP_LEN, HEAD_DIM), 0.5),
        'cmp_w1': nrm(ks[14], (L, 2, CMP_LEN * HEAD_DIM, CMP_HIDDEN), (CMP_LEN * HEAD_DIM) ** -0.5),
        'cmp_b1': nrm(ks[15], (L, 2, CMP_HIDDEN), 0.02),
        'cmp_w2': nrm(ks[16], (L, 2, CMP_HIDDEN, HEAD_DIM), CMP_HIDDEN ** -0.5),
        'cmp_b2': nrm(ks[17], (L, 2, HEAD_DIM), 0.02),
        'ffn2_norm': gain(ks[18], (L, D_MODEL)),
        'ffn2_w_in': nrm(ks[19], (L, D_MODEL, 2 * D_FF), D_MODEL ** -0.5),
        'ffn2_w_out': nrm(ks[20], (L, D_FF, D_MODEL), D_FF ** -0.5),
    }


def reference(x, rel_bias_table, ffn1_norm, ffn1_w_in, ffn1_w_out, mix_norm, w_mix_in, w_mix_out,
              q_norm_a, k_norm_a, sinks_a, q_norm_b, k_norm_b,
              cmp_pos, cmp_w1, cmp_b1, cmp_w2, cmp_b2,
              ffn2_norm, ffn2_w_in, ffn2_w_out):
    for l in range(DEPTH):
        x = hybrid_layer(x, rel_bias_table, ffn1_norm[l], ffn1_w_in[l], ffn1_w_out[l],
                         mix_norm[l], w_mix_in[l], w_mix_out[l],
                         q_norm_a[l], k_norm_a[l], sinks_a[l], q_norm_b[l], k_norm_b[l],
                         cmp_pos[l], cmp_w1[l], cmp_b1[l], cmp_w2[l], cmp_b2[l],
                         ffn2_norm[l], ffn2_w_in[l], ffn2_w_out[l])
    return x
```

```python
import functools
import math

import numpy as np
import jax
import jax.numpy as jnp
from jax import lax
from jax.experimental import pallas as pl
from jax.experimental.pallas import tpu as pltpu

F32 = jnp.float32
BF16 = jnp.bfloat16

D_MODEL = 1024
HEAD_DIM = 64
N_HEADS_A = 8
N_KV_A = 2
N_HEADS_B = 8
N_KV_B = 2
N_HEADS = N_HEADS_A + N_HEADS_B
WINDOW_A = 128
CMP_LEN = 32
CMP_STRIDE = 16
CMP_HIDDEN = 128
SEL_LEN = 64
SEL_TOPK = 16
WINDOW_B = 512
FORCE_BONUS = 1000.0
NUM_BUCKETS = 32
MAX_DISTANCE = 128
D_FF = 2816
EPS = 1e-6
NEG = -1e9

QA = N_HEADS_A * HEAD_DIM
KVA = N_KV_A * HEAD_DIM
QB = N_HEADS_B * HEAD_DIM
KVB = N_KV_B * HEAD_DIM
N_GATES = 3 * N_HEADS_B
OFF_KA = QA
OFF_VA = OFF_KA + KVA
OFF_QB = OFF_VA + KVA
OFF_KVB = OFF_QB + QB
OFF_GB = OFF_KVB + 6 * KVB
D_IN = OFF_GB + N_GATES

LANES = 128
TQ = 128
FAR_TK = 512
FFN_TM = 512
FFN_TF = 256
VMEM_LIMIT = 56 * 1024 * 1024

P_QA = 0
P_QB = P_QA + N_HEADS_A * LANES
P_KA = P_QB + N_HEADS_B * LANES
P_KS = P_KA + LANES
P_KW = P_KS + LANES
P_VA = P_KW + LANES
P_VS = P_VA + LANES
P_VW = P_VS + LANES
P_KVC = P_VW + LANES
P_GATE = P_KVC + 2 * LANES
P_TOTAL = P_GATE + LANES
P_NORM = P_VA


def _nt_dot(a, b):
    return lax.dot_general(a, b, (((1,), (1,)), ((), ())), preferred_element_type=F32)


def _dot(a, b):
    return jnp.dot(a, b, preferred_element_type=F32)


def _rms_rows(x, g):
    ms = jnp.mean(x * x, axis=-1, keepdims=True)
    return x * lax.rsqrt(ms + EPS) * g


def _bucket_np(rel):
    n = np.maximum(rel, 0)
    max_exact = NUM_BUCKETS // 2
    nf = np.maximum(n, 1).astype(np.float64)
    large = max_exact + (np.log(nf / max_exact) / math.log(MAX_DISTANCE / max_exact)
                         * (NUM_BUCKETS - max_exact)).astype(np.int64)
    large = np.minimum(large, NUM_BUCKETS - 1)
    return np.where(n < max_exact, n, large).astype(np.int32)


def _ffn_body(*refs, pre):
    if pre:
        x_ref, ma_ref, mb_ref, wo_ref, nrm_ref, win_ref, wout_ref, o_ref, acc_ref = refs
    else:
        x_ref, nrm_ref, win_ref, wout_ref, o_ref, acc_ref = refs
    x = x_ref[...]
    if pre:
        x = x + _dot(ma_ref[...], wo_ref[0:QA, :]) + _dot(mb_ref[...], wo_ref[QA:QA + QB, :])
    h = _rms_rows(x, nrm_ref[...]).astype(BF16)
    for c in range(D_FF // FFN_TF):
        gate = _dot(h, win_ref[:, c * FFN_TF:(c + 1) * FFN_TF])
        up = _dot(h, win_ref[:, D_FF + c * FFN_TF:D_FF + (c + 1) * FFN_TF])
        a = (gate * jax.nn.sigmoid(gate) * up).astype(BF16)
        part = _dot(a, wout_ref[c * FFN_TF:(c + 1) * FFN_TF, :])
        if c == 0:
            acc_ref[...] = part
        else:
            acc_ref[...] += part
    o_ref[...] = x + 0.5 * acc_ref[...]


def _ffn(x2d, nrm, w_in, w_out, pre=None):
    m = x2d.shape[0]
    tm = FFN_TM
    assert m % tm == 0
    row = lambda i: (i, 0)
    const = lambda i: (0, 0)
    in_specs = [pl.BlockSpec((tm, D_MODEL), row)]
    args = [x2d]
    if pre is not None:
        mix_a, mix_b, wo = pre
        in_specs += [pl.BlockSpec((tm, QA), row), pl.BlockSpec((tm, QB), row),
                     pl.BlockSpec((QA + QB, D_MODEL), const)]
        args += [mix_a, mix_b, wo]
    in_specs += [pl.BlockSpec((1, D_MODEL), const),
                 pl.BlockSpec((D_MODEL, 2 * D_FF), const),
                 pl.BlockSpec((D_FF, D_MODEL), const)]
    args += [nrm.reshape(1, D_MODEL), w_in, w_out]
    return pl.pallas_call(
        functools.partial(_ffn_body, pre=pre is not None),
        out_shape=jax.ShapeDtypeStruct((m, D_MODEL), F32),
        grid=(m // tm,),
        in_specs=in_specs,
        out_specs=pl.BlockSpec((tm, D_MODEL), row),
        scratch_shapes=[pltpu.VMEM((tm, D_MODEL), F32)],
        compiler_params=pltpu.CompilerParams(
            dimension_semantics=("arbitrary",), vmem_limit_bytes=VMEM_LIMIT),
        name="ffn_pre" if pre is not None else "ffn",
    )(*args)


def _proj_body(x_ref, nrm_ref, w_ref, gain_ref, qa_ref, qb_ref, ka_ref, ks_ref, kw_ref,
               va_ref, vs_ref, vw_ref, kvc_ref, gate_ref):
    tm = x_ref.shape[0]
    h = _rms_rows(x_ref[...], nrm_ref[...]).astype(BF16)
    lo = lax.broadcasted_iota(jnp.int32, (tm, LANES), 1) < HEAD_DIM
    inv_d = 1.0 / HEAD_DIM

    def pair(c0):
        p2 = _dot(h, w_ref[:, c0:c0 + 2 * LANES])
        return p2[:, :LANES], p2[:, LANES:]

    def one_head(p, c0):
        ms = jnp.sum(p * p, axis=-1, keepdims=True) * inv_d
        return (p * lax.rsqrt(ms + EPS) * gain_ref[:, c0:c0 + LANES]).astype(BF16)

    def two_heads(p, c0):
        sq = p * p
        ms0 = jnp.sum(jnp.where(lo, sq, 0.0), axis=-1, keepdims=True) * inv_d
        ms1 = jnp.sum(jnp.where(lo, 0.0, sq), axis=-1, keepdims=True) * inv_d
        r = jnp.where(lo, lax.rsqrt(ms0 + EPS), lax.rsqrt(ms1 + EPS))
        return (p * r * gain_ref[:, c0:c0 + LANES]).astype(BF16)

    for hp in range(N_HEADS // 2):
        c0 = 2 * hp * LANES
        pa, pb = pair(c0)
        ref = qa_ref if c0 < P_QB else qb_ref
        l0 = c0 - (P_QA if c0 < P_QB else P_QB)
        ref[:, l0:l0 + LANES] = one_head(pa, c0)
        ref[:, l0 + LANES:l0 + 2 * LANES] = one_head(pb, c0 + LANES)
    pa, pb = pair(P_KA)
    ka_ref[...] = two_heads(pa, P_KA)
    ks_ref[...] = two_heads(pb, P_KS)
    pa, pb = pair(P_KW)
    kw_ref[...] = two_heads(pa, P_KW)
    va_ref[...] = pb.astype(BF16)
    pa, pb = pair(P_VS)
    vs_ref[...] = pa.astype(BF16)
    vw_ref[...] = pb.astype(BF16)
    pa, pb = pair(P_KVC)
    kvc_ref[:, :LANES] = pa
    kvc_ref[:, LANES:] = pb
    pg = _dot(h, w_ref[:, P_GATE:P_GATE + LANES])
    gate_ref[...] = jax.nn.sigmoid(pg)


def _proj_src_columns():
    src = np.full((P_TOTAL,), -1, np.int64)
    gidx = np.full((P_NORM,), -1, np.int64)
    d = np.arange(HEAD_DIM)
    for hh in range(N_HEADS_A):
        g = hh // (N_HEADS_A // N_KV_A)
        src[P_QA + hh * LANES + g * HEAD_DIM + d] = hh * HEAD_DIM + d
        gidx[P_QA + hh * LANES + g * HEAD_DIM + d] = d
    for hh in range(N_HEADS_B):
        g = hh // (N_HEADS_B // N_KV_B)
        src[P_QB + hh * LANES + g * HEAD_DIM + d] = OFF_QB + hh * HEAD_DIM + d
        gidx[P_QB + hh * LANES + g * HEAD_DIM + d] = HEAD_DIM + d
    two = np.arange(2 * HEAD_DIM)
    src[P_KA + two] = OFF_KA + two
    gidx[P_KA + two] = 2 * HEAD_DIM + two % HEAD_DIM
    src[P_VA + two] = OFF_VA + two
    for slot, dst in ((0, P_KVC), (1, P_KVC + LANES), (2, P_KS), (3, P_VS), (4, P_KW), (5, P_VW)):
        src[dst + two] = OFF_KVB + slot * KVB + two
    gidx[P_KS + two] = 3 * HEAD_DIM + two % HEAD_DIM
    gidx[P_KW + two] = 3 * HEAD_DIM + two % HEAD_DIM
    src[P_GATE + np.arange(N_GATES)] = OFF_GB + np.arange(N_GATES)
    return src, gidx


def _proj(x2d, nrm, w_mix_in, q_norm_a, k_norm_a, q_norm_b, k_norm_b):
    m = x2d.shape[0]
    tm = FFN_TM
    src, gidx = _proj_src_columns()
    w = jnp.where(jnp.asarray(src >= 0)[None, :], w_mix_in[:, np.maximum(src, 0)], 0.0).astype(BF16)
    scale = HEAD_DIM ** -0.5
    gsrc = jnp.concatenate([q_norm_a * scale, q_norm_b * scale, k_norm_a, k_norm_b]).astype(F32)
    gain = jnp.where(jnp.asarray(gidx >= 0), gsrc[np.maximum(gidx, 0)], 0.0).reshape(1, P_NORM)
    row = lambda i: (i, 0)
    const = lambda i: (0, 0)
    widths = [(N_HEADS_A * LANES, BF16), (N_HEADS_B * LANES, BF16)] + [(LANES, BF16)] * 6 + \
             [(2 * LANES, F32), (LANES, F32)]
    return pl.pallas_call(
        _proj_body,
        out_shape=[jax.ShapeDtypeStruct((m, wd), dt) for wd, dt in widths],
        grid=(m // tm,),
        in_specs=[pl.BlockSpec((tm, D_MODEL), row), pl.BlockSpec((1, D_MODEL), const),
                  pl.BlockSpec((D_MODEL, P_TOTAL), const), pl.BlockSpec((1, P_NORM), const)],
        out_specs=[pl.BlockSpec((tm, wd), row) for wd, _ in widths],
        compiler_params=pltpu.CompilerParams(
            dimension_semantics=("arbitrary",), vmem_limit_bytes=VMEM_LIMIT),
        name="proj",
    )(x2d, nrm.reshape(1, D_MODEL), w, gain)


CMP_FEAT = CMP_STRIDE * 2 * LANES
CMP_SLOTS = 4


def _compress_body(x_ref, pos_ref, w1_ref, b1_ref, w2_ref, b2_ref, gain_ref, kc_ref, vc_ref):
    nck = x_ref.shape[1]
    x = x_ref[0]
    a = _dot((x + pos_ref[0]).astype(BF16), w1_ref[0])
    b = _dot((x + pos_ref[1]).astype(BF16), w1_ref[1])
    pre = a + pltpu.roll(b, nck - 1, 0) + b1_ref[...]
    hid = (pre * jax.nn.sigmoid(pre)).astype(BF16)
    out = _dot(hid, w2_ref[...]) + b2_ref[...]
    k = out[:, :LANES]
    lo = lax.broadcasted_iota(jnp.int32, (nck, LANES), 1) < HEAD_DIM
    sq = k * k
    ms0 = jnp.sum(jnp.where(lo, sq, 0.0), axis=-1, keepdims=True) * (1.0 / HEAD_DIM)
    ms1 = jnp.sum(jnp.where(lo, 0.0, sq), axis=-1, keepdims=True) * (1.0 / HEAD_DIM)
    r = jnp.where(lo, lax.rsqrt(ms0 + EPS), lax.rsqrt(ms1 + EPS))
    kc_ref[0] = (k * r * gain_ref[...]).astype(BF16)
    vc_ref[0] = out[:, LANES:].astype(BF16)


def _compress(kvc, cmp_pos, cmp_w1, cmp_b1, cmp_w2, cmp_b2, k_norm_b):
    b, s, _ = kvc.shape
    nck = s // CMP_STRIDE
    x = kvc.reshape(b, nck, CMP_FEAT)
    r = CMP_LEN // CMP_STRIDE
    w1r = cmp_w1.reshape(2, r, CMP_STRIDE, HEAD_DIM, CMP_HIDDEN)
    eye = jnp.eye(CMP_SLOTS, dtype=F32)
    kv_of_slot = np.array([0, 0, 1, 1])
    w1s = w1r[kv_of_slot]
    w1big = jnp.einsum('sjpdh,st->jpsdth', w1s, eye).reshape(r, CMP_FEAT, CMP_SLOTS * CMP_HIDDEN)
    posr = cmp_pos.reshape(2, r, CMP_STRIDE, HEAD_DIM)[kv_of_slot]
    posbig = posr.transpose(1, 2, 0, 3).reshape(r, 1, CMP_FEAT)
    b1big = cmp_b1[kv_of_slot].reshape(1, CMP_SLOTS * CMP_HIDDEN)
    w2big = jnp.einsum('shd,st->shtd', cmp_w2[kv_of_slot], eye).reshape(
        CMP_SLOTS * CMP_HIDDEN, CMP_SLOTS * HEAD_DIM)
    b2big = cmp_b2[kv_of_slot].reshape(1, CMP_SLOTS * HEAD_DIM)
    gain = jnp.tile(k_norm_b, 2).reshape(1, LANES)
    assert r == 2
    c3 = lambda i: (0, 0, 0)
    c2 = lambda i: (0, 0)
    return pl.pallas_call(
        _compress_body,
        out_shape=[jax.ShapeDtypeStruct((b, nck, LANES), BF16)] * 2,
        grid=(b,),
        in_specs=[pl.BlockSpec((1, nck, CMP_FEAT), lambda i: (i, 0, 0)),
                  pl.BlockSpec((r, 1, CMP_FEAT), c3),
                  pl.BlockSpec((r, CMP_FEAT, CMP_SLOTS * CMP_HIDDEN), c3),
                  pl.BlockSpec((1, CMP_SLOTS * CMP_HIDDEN), c2),
                  pl.BlockSpec((CMP_SLOTS * CMP_HIDDEN, CMP_SLOTS * HEAD_DIM), c2),
                  pl.BlockSpec((1, CMP_SLOTS * HEAD_DIM), c2),
                  pl.BlockSpec((1, LANES), c2)],
        out_specs=[pl.BlockSpec((1, nck, LANES), lambda i: (i, 0, 0))] * 2,
        compiler_params=pltpu.CompilerParams(
            dimension_semantics=("arbitrary",), vmem_limit_bytes=VMEM_LIMIT),
        name="compress",
    )(x, posbig, w1big.astype(BF16), b1big, w2big.astype(BF16), b2big, gain)


def _stack_heads(q_ref, g, heads_per_group):
    return jnp.concatenate(
        [q_ref[0, :, (g * heads_per_group + r) * LANES:(g * heads_per_group + r + 1) * LANES]
         for r in range(heads_per_group)], axis=0)


def _merge_heads(o_heads, g, lane):
    tiles = []
    for c in range(len(o_heads) // 2):
        lo_src = o_heads[2 * c] if g == 0 else pltpu.roll(o_heads[2 * c], HEAD_DIM, 1)
        hi_src = o_heads[2 * c + 1] if g == 1 else pltpu.roll(o_heads[2 * c + 1], HEAD_DIM, 1)
        tiles.append(jnp.where(lane < HEAD_DIM, lo_src, hi_src))
    return tiles


def _band_tiles(tbl):
    i = np.arange(TQ)[:, None]
    j = np.arange(TQ)[None, :]
    prev = tbl[_bucket_np(i - j + TQ)]
    diag = tbl[_bucket_np(i - j)]
    return prev.transpose(2, 0, 1).astype(F32), diag.transpose(2, 0, 1).astype(F32)


def _swa_body(sink_ref, q_ref, k_ref, v_ref, bias_ref, o_ref):
    qt = pl.program_id(1)
    t0 = qt * TQ
    s_prev = pl.multiple_of(jnp.maximum(t0 - TQ, 0), TQ)
    s_diag = pl.multiple_of(t0, TQ)
    k2 = jnp.concatenate([k_ref[0, pl.ds(s_prev, TQ), :], k_ref[0, pl.ds(s_diag, TQ), :]], axis=0)
    v2 = jnp.concatenate([v_ref[0, pl.ds(s_prev, TQ), :], v_ref[0, pl.ds(s_diag, TQ), :]], axis=0)
    ii = lax.broadcasted_iota(jnp.int32, (TQ, 2 * TQ), 0)
    jj = lax.broadcasted_iota(jnp.int32, (TQ, 2 * TQ), 1)
    mask = ((jj < TQ) & (jj > ii) & (qt > 0)) | ((jj >= TQ) & ((jj - TQ) <= ii))
    lane = lax.broadcasted_iota(jnp.int32, (TQ, LANES), 1)
    rpg = N_HEADS_A // N_KV_A
    for g in range(N_KV_A):
        q4 = _stack_heads(q_ref, g, rpg)
        s_all = _nt_dot(q4, k2)
        ps, dens = [], []
        for r in range(rpg):
            hh = g * rpg + r
            s = jnp.where(mask, s_all[r * TQ:(r + 1) * TQ] + bias_ref[hh], NEG)
            sink = sink_ref[hh]
            m = jnp.maximum(jnp.max(s, axis=-1, keepdims=True), sink)
            p = jnp.exp(s - m)
            dens.append(jnp.sum(p, axis=-1, keepdims=True) + jnp.exp(sink - m))
            ps.append(p.astype(BF16))
        o_all = _dot(jnp.concatenate(ps, axis=0), v2)
        o_heads = [o_all[r * TQ:(r + 1) * TQ] / dens[r] for r in range(rpg)]
        for c, tile in enumerate(_merge_heads(o_heads, g, lane)):
            col = (g * rpg // 2 + c) * LANES
            o_ref[0, :, col:col + LANES] = tile.astype(BF16)


def _swa(qa, ka, va, bias, sinks):
    b, s, _ = qa.shape
    assert WINDOW_A == TQ
    return pl.pallas_call(
        _swa_body,
        out_shape=jax.ShapeDtypeStruct((b, s, QA), BF16),
        grid=(b, s // TQ),
        in_specs=[pl.BlockSpec(memory_space=pltpu.SMEM),
                  pl.BlockSpec((1, TQ, N_HEADS_A * LANES), lambda i, j: (i, j, 0)),
                  pl.BlockSpec((1, s, LANES), lambda i, j: (i, 0, 0)),
                  pl.BlockSpec((1, s, LANES), lambda i, j: (i, 0, 0)),
                  pl.BlockSpec((N_HEADS_A, TQ, 2 * TQ), lambda i, j: (0, 0, 0))],
        out_specs=pl.BlockSpec((1, TQ, QA), lambda i, j: (i, j, 0)),
        compiler_params=pltpu.CompilerParams(
            dimension_semantics=("arbitrary", "arbitrary"), vmem_limit_bytes=VMEM_LIMIT),
        name="swa",
    )(sinks.astype(F32), qa, ka, va, bias)


CMP_NEAR = 16
CMP_NEAR_LO = 9


def _nsa_body(q_ref, kc_ref, vc_ref, ks_ref, vs_ref, kw_ref, vw_ref, oh_ref, gate_ref,
              bprev_ref, bdiag_ref, g2_ref, ovl_ref, o_ref, m_ref, acc_ref, *, n_slc, n_top):
    qt = pl.program_id(1)
    t0 = qt * TQ
    rpg = N_HEADS_B // N_KV_B
    rows = rpg * TQ
    ncmp = kc_ref.shape[1]
    nwb = WINDOW_B // TQ
    s_prev = pl.multiple_of(jnp.maximum(t0 - TQ, 0), TQ)
    s_diag = pl.multiple_of(t0, TQ)
    lane = lax.broadcasted_iota(jnp.int32, (TQ, LANES), 1)
    rowi = lax.broadcasted_iota(jnp.int32, (TQ, LANES), 0)
    causal = lane <= rowi
    ones_col = (lane == 0).astype(BF16)
    has_prev = qt > 0

    c_idx = lax.broadcasted_iota(jnp.int32, (ncmp, LANES), 0)
    l_idx = lax.broadcasted_iota(jnp.int32, (ncmp, LANES), 1)
    place = ((c_idx - (TQ // CMP_STRIDE) * qt + CMP_NEAR_LO) == (l_idx & (CMP_NEAR - 1))) \
        & (l_idx < 2 * CMP_NEAR)
    kc_aug = jnp.concatenate([kc_ref[0], place.astype(BF16)], axis=1)
    qi = lax.broadcasted_iota(jnp.int32, (TQ, ncmp), 0)
    ci = lax.broadcasted_iota(jnp.int32, (TQ, ncmp), 1)
    mask_c = (t0 + qi - CMP_STRIDE * ci - (CMP_LEN - 1)) >= 0
    vc = vc_ref[0]

    cur = (t0 + rowi) // SEL_LEN
    valid = (lane <= cur) & (lane < n_slc)
    forced = (lane == 0) | (lane == cur) | (lane == cur - 1)
    n_io = lax.broadcasted_iota(jnp.int32, (n_slc, TQ), 0)
    cur0 = t0 // SEL_LEN
    near_lo = cur0 - TQ // SEL_LEN

    ks2 = jnp.concatenate(
        [jnp.concatenate([ks_ref[0, pl.ds(s_prev, TQ), :], oh_ref[pl.ds(s_prev, TQ), :]], axis=1),
         jnp.concatenate([ks_ref[0, pl.ds(s_diag, TQ), :], oh_ref[pl.ds(s_diag, TQ), :]], axis=1)],
        axis=0)
    vs2 = jnp.concatenate(
        [jnp.concatenate([vs_ref[0, pl.ds(s_prev, TQ), :], ones_col], axis=1),
         jnp.concatenate([vs_ref[0, pl.ds(s_diag, TQ), :], ones_col], axis=1)], axis=0)
    ones_far = jnp.concatenate([ones_col] * (FAR_TK // TQ), axis=0)

    w_starts = [pl.multiple_of(jnp.maximum(t0 - kb * TQ, 0), TQ) for kb in range(nwb, -1, -1)]
    kw5 = jnp.concatenate([kw_ref[0, pl.ds(st, TQ), :] for st in w_starts], axis=0)
    vw5 = jnp.concatenate([vw_ref[0, pl.ds(st, TQ), :] for st in w_starts], axis=0)

    for g in range(N_KV_B):
        q4 = _stack_heads(q_ref, g, rpg)

        lc_all = _nt_dot(jnp.concatenate([q4, g2_ref[g]], axis=1), kc_aug)
        pcs = None
        pc_rows = []
        for r in range(rpg):
            lc = jnp.where(mask_c, lc_all[r * TQ:(r + 1) * TQ], NEG)
            e = jnp.where(mask_c, jnp.exp(lc - jnp.max(lc, axis=-1, keepdims=True)), 0.0)
            den = jnp.maximum(jnp.sum(e, axis=-1, keepdims=True), jnp.finfo(F32).tiny)
            pc = e / den
            pcs = pc if pcs is None else pcs + pc
            pc_rows.append(pc.astype(BF16))
        o_cmp = _dot(jnp.concatenate(pc_rows, axis=0), vc)

        pcs_hi = pcs.astype(BF16)
        pcs_lo = (pcs - pcs_hi.astype(F32)).astype(BF16)
        imp = _dot(pcs_hi, ovl_ref[...]) + _dot(pcs_lo, ovl_ref[...])
        score = jnp.where(valid, imp + jnp.where(forced, FORCE_BONUS, 0.0), NEG)
        s_t = score.T[:n_slc]
        rank = jnp.zeros((n_slc, TQ), F32)
        for mm in range(n_slc):
            sm = s_t[mm:mm + 1, :]
            beats = (sm > s_t) | ((sm == s_t) & (n_io > mm))
            rank = rank + jnp.where(beats, 1.0, 0.0)
        sel_t = jnp.where((rank < n_top) & (s_t > 0.5 * NEG), 1.0, 0.0)
        if n_slc < LANES:
            sel_t = jnp.concatenate([sel_t, jnp.zeros((LANES - n_slc, TQ), F32)], axis=0)
        sel = sel_t.T > 0.5
        m_near = jnp.where(sel & (lane >= near_lo), 0.0, NEG).astype(BF16)
        m_far = jnp.where(sel & (lane < near_lo), 0.0, NEG).astype(BF16)
        q_near = jnp.concatenate([q4, jnp.concatenate([m_near] * rpg, axis=0)], axis=1)
        q_far = jnp.concatenate([q4, jnp.concatenate([m_far] * rpg, axis=0)], axis=1)

        s_all = _nt_dot(q_near, ks2)
        s_rows = []
        for r in range(rpg):
            hh = g * rpg + r
            sr = s_all[r * TQ:(r + 1) * TQ]
            s_p = jnp.where(has_prev, sr[:, :TQ] + bprev_ref[hh], NEG)
            s_d = jnp.where(causal, sr[:, TQ:] + bdiag_ref[hh], NEG)
            s_rows.append(jnp.concatenate([s_p, s_d], axis=1))
        s_near = jnp.concatenate(s_rows, axis=0)
        m0 = jnp.max(s_near, axis=-1, keepdims=True)
        m_ref[...] = m0
        acc_ref[...] = _dot(jnp.exp(s_near - m0).astype(BF16), vs2)

        def far_step(c, carry):
            k0 = pl.multiple_of(c * FAR_TK, FAR_TK)
            kk = jnp.concatenate([ks_ref[0, pl.ds(k0, FAR_TK), :], oh_ref[pl.ds(k0, FAR_TK), :]], axis=1)
            vv = jnp.concatenate([vs_ref[0, pl.ds(k0, FAR_TK), :], ones_far], axis=1)
            s = _nt_dot(q_far, kk)
            m_old = m_ref[...]
            m_new = jnp.maximum(m_old, jnp.max(s, axis=-1, keepdims=True))
            alpha = jnp.exp(m_old - m_new)
            acc_ref[...] = acc_ref[...] * alpha + _dot(jnp.exp(s - m_new).astype(BF16), vv)
            m_ref[...] = m_new
            return carry

        lax.fori_loop(0, (qt + FAR_TK // TQ - 2) // (FAR_TK // TQ), far_step, 0)
        acc = acc_ref[...]
        o_sel = acc[:, :LANES] / acc[:, LANES:LANES + 1]

        w_all = _nt_dot(q4, kw5)
        pw, lw = [], []
        for r in range(rpg):
            hh = g * rpg + r
            wr = w_all[r * TQ:(r + 1) * TQ]
            blocks = []
            for bi, kb in enumerate(range(nwb, -1, -1)):
                blk = wr[:, bi * TQ:(bi + 1) * TQ]
                if kb == nwb:
                    blk = jnp.where((lane > rowi) & (qt >= kb), blk, NEG)
                elif kb >= 2:
                    blk = jnp.where(qt >= kb, blk, NEG)
                elif kb == 1:
                    blk = jnp.where(has_prev, blk + bprev_ref[hh], NEG)
                else:
                    blk = jnp.where(causal, blk + bdiag_ref[hh], NEG)
                blocks.append(blk)
            sw = jnp.concatenate(blocks, axis=1)
            p = jnp.exp(sw - jnp.max(sw, axis=-1, keepdims=True))
            lw.append(jnp.sum(p, axis=-1, keepdims=True))
            pw.append(p.astype(BF16))
        o_win = _dot(jnp.concatenate(pw, axis=0), vw5)

        o_heads = []
        for r in range(rpg):
            hh = g * rpg + r
            sl = slice(r * TQ, (r + 1) * TQ)
            g_cmp = gate_ref[0, :, hh:hh + 1]
            g_sel = gate_ref[0, :, N_HEADS_B + hh:N_HEADS_B + hh + 1]
            g_win = gate_ref[0, :, 2 * N_HEADS_B + hh:2 * N_HEADS_B + hh + 1]
            o_heads.append(g_cmp * o_cmp[sl] + g_sel * o_sel[sl] + g_win * (o_win[sl] / lw[r]))
        for c, tile in enumerate(_merge_heads(o_heads, g, lane)):
            col = (g * rpg // 2 + c) * LANES
            o_ref[0, :, col:col + LANES] = tile.astype(BF16)


def _nsa(qb, kc, vc, ks, vs, kw, vw, gates, bprev, bdiag, g2):
    b, s, _ = qb.shape
    assert s % FAR_TK == 0 and WINDOW_B % TQ == 0 and TQ % SEL_LEN == 0
    n_slc = s // SEL_LEN
    assert n_slc <= LANES and n_slc % 8 == 0
    n_top = min(SEL_TOPK, n_slc)
    ncmp = s // CMP_STRIDE
    rpg = N_HEADS_B // N_KV_B
    kpos = np.arange(s)
    onehot = (kpos[:, None] // SEL_LEN == np.arange(LANES)[None, :]).astype(np.float32)
    cstart = np.arange(ncmp) * CMP_STRIDE
    sstart = np.arange(LANES) * SEL_LEN
    ovl = ((cstart[:, None] < sstart[None, :] + SEL_LEN)
           & (cstart[:, None] + CMP_LEN - 1 >= sstart[None, :])
           & (np.arange(ncmp)[:, None] < ncmp - CMP_LEN // CMP_STRIDE + 1)
           & (np.arange(LANES)[None, :] < n_slc)).astype(np.float32)
    full = lambda i, j: (i, 0, 0)
    return pl.pallas_call(
        functools.partial(_nsa_body, n_slc=n_slc, n_top=n_top),
        out_shape=jax.ShapeDtypeStruct((b, s, QB), BF16),
        grid=(b, s // TQ),
        in_specs=[pl.BlockSpec((1, TQ, N_HEADS_B * LANES), lambda i, j: (i, j, 0)),
                  pl.BlockSpec((1, ncmp, LANES), full), pl.BlockSpec((1, ncmp, LANES), full),
                  pl.BlockSpec((1, s, LANES), full), pl.BlockSpec((1, s, LANES), full),
                  pl.BlockSpec((1, s, LANES), full), pl.BlockSpec((1, s, LANES), full),
                  pl.BlockSpec((s, LANES), lambda i, j: (0, 0)),
                  pl.BlockSpec((1, TQ, LANES), lambda i, j: (i, j, 0)),
                  pl.BlockSpec((N_HEADS_B, TQ, TQ), lambda i, j: (0, 0, 0)),
                  pl.BlockSpec((N_HEADS_B, TQ, TQ), lambda i, j: (0, 0, 0)),
                  pl.BlockSpec((N_KV_B, rpg * TQ, LANES), lambda i, j: (0, 0, 0)),
                  pl.BlockSpec((ncmp, LANES), lambda i, j: (0, 0))],
        out_specs=pl.BlockSpec((1, TQ, QB), lambda i, j: (i, j, 0)),
        scratch_shapes=[pltpu.VMEM((rpg * TQ, 1), F32), pltpu.VMEM((rpg * TQ, 2 * LANES), F32)],
        compiler_params=pltpu.CompilerParams(
            dimension_semantics=("arbitrary", "arbitrary"), vmem_limit_bytes=VMEM_LIMIT),
        name="nsa",
    )(qb, kc, vc, ks, vs, kw, vw, jnp.asarray(onehot, BF16), gates, bprev, bdiag, g2,
      jnp.asarray(ovl, BF16))


def _cmp_bias_operand(tbl_b):
    i = np.arange(TQ)[:, None]
    u = np.arange(CMP_NEAR)[None, :]
    rel = i - CMP_STRIDE * (u - CMP_NEAR_LO) - (CMP_LEN - 1)
    assert rel[0, 0] + CMP_STRIDE >= MAX_DISTANCE and rel[-1, -1] - CMP_STRIDE < 0
    near = (rel >= 0) & (rel < MAX_DISTANCE)
    far_const = tbl_b[NUM_BUCKETS - 1]
    vals = jnp.where(jnp.asarray(near)[:, :, None], tbl_b[_bucket_np(rel)] - far_const, 0.0)
    vals = vals.transpose(2, 0, 1).astype(F32)
    hi = vals.astype(BF16)
    lo = (vals - hi.astype(F32)).astype(BF16)
    pad = jnp.zeros((N_HEADS_B, TQ, LANES - 2 * CMP_NEAR), BF16)
    out = jnp.concatenate([hi, lo, pad], axis=-1)
    rpg = N_HEADS_B // N_KV_B
    return out.reshape(N_KV_B, rpg * TQ, LANES)


def _layer(x, tbl, ffn1_norm, ffn1_w_in, ffn1_w_out, mix_norm, w_mix_in, w_mix_out,
           q_norm_a, k_norm_a, sinks_a, q_norm_b, k_norm_b,
           cmp_pos, cmp_w1, cmp_b1, cmp_w2, cmp_b2, ffn2_norm, ffn2_w_in, ffn2_w_out):
    b, s, d = x.shape
    x2d = x.reshape(b * s, d)
    x1 = _ffn(x2d, ffn1_norm, ffn1_w_in.astype(BF16), ffn1_w_out.astype(BF16))
    qa, qb, ka, ks, kw, va, vs, vw, kvc, gates = _proj(
        x1, mix_norm, w_mix_in, q_norm_a, k_norm_a, q_norm_b, k_norm_b)
    r3 = lambda a: a.reshape(b, s, a.shape[-1])
    kc, vc = _compress(r3(kvc), cmp_pos, cmp_w1, cmp_b1, cmp_w2, cmp_b2, k_norm_b)

    prev_t, diag_t = _band_tiles(tbl)
    bias_a = jnp.concatenate([prev_t[:N_HEADS_A], diag_t[:N_HEADS_A]], axis=-1)
    far_b = tbl[NUM_BUCKETS - 1, N_HEADS_A:].astype(F32)[:, None, None]
    bprev_b = prev_t[N_HEADS_A:] - far_b
    bdiag_b = diag_t[N_HEADS_A:] - far_b
    g2 = _cmp_bias_operand(tbl[:, N_HEADS_A:])

    mix_a = _swa(r3(qa), r3(ka), r3(va), bias_a, sinks_a)
    mix_b = _nsa(r3(qb), kc, vc, r3(ks), r3(vs), r3(kw), r3(vw), r3(gates), bprev_b, bdiag_b, g2)
    y = _ffn(x1, ffn2_norm, ffn2_w_in.astype(BF16), ffn2_w_out.astype(BF16),
             pre=(mix_a.reshape(b * s, QA), mix_b.reshape(b * s, QB), w_mix_out.astype(BF16)))
    return y.reshape(b, s, d)


def kernel(x, rel_bias_table, ffn1_norm, ffn1_w_in, ffn1_w_out, mix_norm, w_mix_in, w_mix_out,
           q_norm_a, k_norm_a, sinks_a, q_norm_b, k_norm_b,
           cmp_pos, cmp_w1, cmp_b1, cmp_w2, cmp_b2, ffn2_norm, ffn2_w_in, ffn2_w_out):
    for l in range(ffn1_norm.shape[0]):
        x = _layer(x, rel_bias_table, ffn1_norm[l], ffn1_w_in[l], ffn1_w_out[l],
                   mix_norm[l], w_mix_in[l], w_mix_out[l],
                   q_norm_a[l], k_norm_a[l], sinks_a[l], q_norm_b[l], k_norm_b[l],
                   cmp_pos[l], cmp_w1[l], cmp_b1[l], cmp_w2[l], cmp_b2[l],
                   ffn2_norm[l], ffn2_w_in[l], ffn2_w_out[l])
    return x
```

```python
import functools
import math

import numpy as np
import jax
import jax.numpy as jnp
from jax import lax
from jax.experimental import pallas as pl
from jax.experimental.pallas import tpu as pltpu

F32 = jnp.float32
BF16 = jnp.bfloat16

D_MODEL = 1024
HEAD_DIM = 64
N_HEADS_A = 8
N_KV_A = 2
N_HEADS_B = 8
N_KV_B = 2
N_HEADS = N_HEADS_A + N_HEADS_B
WINDOW_A = 128
CMP_LEN = 32
CMP_STRIDE = 16
CMP_HIDDEN = 128
SEL_LEN = 64
SEL_TOPK = 16
WINDOW_B = 512
FORCE_BONUS = 1000.0
NUM_BUCKETS = 32
MAX_DISTANCE = 128
D_FF = 2816
EPS = 1e-6
NEG = -1e9

QA = N_HEADS_A * HEAD_DIM
KVA = N_KV_A * HEAD_DIM
QB = N_HEADS_B * HEAD_DIM
KVB = N_KV_B * HEAD_DIM
N_GATES = 3 * N_HEADS_B
OFF_KA = QA
OFF_VA = OFF_KA + KVA
OFF_QB = OFF_VA + KVA
OFF_KVB = OFF_QB + QB
OFF_GB = OFF_KVB + 6 * KVB
D_IN = OFF_GB + N_GATES

LANES = 128
TQ = 128
FAR_TK = 512
FFN_TM = 512
FFN_TF = 256
VMEM_LIMIT = 56 * 1024 * 1024

P_QA = 0
P_QB = P_QA + N_HEADS_A * LANES
P_KA = P_QB + N_HEADS_B * LANES
P_KS = P_KA + LANES
P_KW = P_KS + LANES
P_VA = P_KW + LANES
P_VS = P_VA + LANES
P_VW = P_VS + LANES
P_KVC = P_VW + LANES
P_GATE = P_KVC + 2 * LANES
P_TOTAL = P_GATE + LANES
P_NORM = P_VA


def _nt_dot(a, b):
    return lax.dot_general(a, b, (((1,), (1,)), ((), ())), preferred_element_type=F32)


def _dot(a, b):
    return jnp.dot(a, b, preferred_element_type=F32)


def _rms_rows(x, g):
    ms = jnp.mean(x * x, axis=-1, keepdims=True)
    return x * lax.rsqrt(ms + EPS) * g


def _bucket_np(rel):
    n = np.maximum(rel, 0)
    max_exact = NUM_BUCKETS // 2
    nf = np.maximum(n, 1).astype(np.float64)
    large = max_exact + (np.log(nf / max_exact) / math.log(MAX_DISTANCE / max_exact)
                         * (NUM_BUCKETS - max_exact)).astype(np.int64)
    large = np.minimum(large, NUM_BUCKETS - 1)
    return np.where(n < max_exact, n, large).astype(np.int32)


def _ffn_body(*refs, pre):
    if pre:
        x_ref, ma_ref, mb_ref, wo_ref, nrm_ref, win_ref, wout_ref, o_ref, acc_ref = refs
    else:
        x_ref, nrm_ref, win_ref, wout_ref, o_ref, acc_ref = refs
    x = x_ref[...]
    if pre:
        x = x + _dot(ma_ref[...], wo_ref[0:QA, :]) + _dot(mb_ref[...], wo_ref[QA:QA + QB, :])
    h = _rms_rows(x, nrm_ref[...]).astype(BF16)
    for c in range(D_FF // FFN_TF):
        gate = _dot(h, win_ref[:, c * FFN_TF:(c + 1) * FFN_TF])
        up = _dot(h, win_ref[:, D_FF + c * FFN_TF:D_FF + (c + 1) * FFN_TF])
        a = (gate * jax.nn.sigmoid(gate) * up).astype(BF16)
        part = _dot(a, wout_ref[c * FFN_TF:(c + 1) * FFN_TF, :])
        if c == 0:
            acc_ref[...] = part
        else:
            acc_ref[...] += part
    o_ref[...] = x + 0.5 * acc_ref[...]


def _ffn(x2d, nrm, w_in, w_out, pre=None):
    m = x2d.shape[0]
    tm = FFN_TM
    assert m % tm == 0
    row = lambda i: (i, 0)
    const = lambda i: (0, 0)
    in_specs = [pl.BlockSpec((tm, D_MODEL), row)]
    args = [x2d]
    if pre is not None:
        mix_a, mix_b, wo = pre
        in_specs += [pl.BlockSpec((tm, QA), row), pl.BlockSpec((tm, QB), row),
                     pl.BlockSpec((QA + QB, D_MODEL), const)]
        args += [mix_a, mix_b, wo]
    in_specs += [pl.BlockSpec((1, D_MODEL), const),
                 pl.BlockSpec((D_MODEL, 2 * D_FF), const),
                 pl.BlockSpec((D_FF, D_MODEL), const)]
    args += [nrm.reshape(1, D_MODEL), w_in, w_out]
    return pl.pallas_call(
        functools.partial(_ffn_body, pre=pre is not None),
        out_shape=jax.ShapeDtypeStruct((m, D_MODEL), F32),
        grid=(m // tm,),
        in_specs=in_specs,
        out_specs=pl.BlockSpec((tm, D_MODEL), row),
        scratch_shapes=[pltpu.VMEM((tm, D_MODEL), F32)],
        compiler_params=pltpu.CompilerParams(
            dimension_semantics=("arbitrary",), vmem_limit_bytes=VMEM_LIMIT),
        name="ffn_pre" if pre is not None else "ffn",
    )(*args)


def _proj_body(x_ref, nrm_ref, w_ref, gain_ref, qa_ref, qb_ref, ka_ref, ks_ref, kw_ref,
               va_ref, vs_ref, vw_ref, kvc_ref, gate_ref):
    tm = x_ref.shape[0]
    h = _rms_rows(x_ref[...], nrm_ref[...]).astype(BF16)
    lo = lax.broadcasted_iota(jnp.int32, (tm, LANES), 1) < HEAD_DIM
    inv_d = 1.0 / HEAD_DIM

    def pair(c0):
        p2 = _dot(h, w_ref[:, c0:c0 + 2 * LANES])
        return p2[:, :LANES], p2[:, LANES:]

    def one_head(p, c0):
        ms = jnp.sum(p * p, axis=-1, keepdims=True) * inv_d
        return (p * lax.rsqrt(ms + EPS) * gain_ref[:, c0:c0 + LANES]).astype(BF16)

    def two_heads(p, c0):
        sq = p * p
        ms0 = jnp.sum(jnp.where(lo, sq, 0.0), axis=-1, keepdims=True) * inv_d
        ms1 = jnp.sum(jnp.where(lo, 0.0, sq), axis=-1, keepdims=True) * inv_d
        r = jnp.where(lo, lax.rsqrt(ms0 + EPS), lax.rsqrt(ms1 + EPS))
        return (p * r * gain_ref[:, c0:c0 + LANES]).astype(BF16)

    for hp in range(N_HEADS // 2):
        c0 = 2 * hp * LANES
        pa, pb = pair(c0)
        ref = qa_ref if c0 < P_QB else qb_ref
        l0 = c0 - (P_QA if c0 < P_QB else P_QB)
        ref[:, l0:l0 + LANES] = one_head(pa, c0)
        ref[:, l0 + LANES:l0 + 2 * LANES] = one_head(pb, c0 + LANES)
    pa, pb = pair(P_KA)
    ka_ref[...] = two_heads(pa, P_KA)
    ks_ref[...] = two_heads(pb, P_KS)
    pa, pb = pair(P_KW)
    kw_ref[...] = two_heads(pa, P_KW)
    va_ref[...] = pb.astype(BF16)
    pa, pb = pair(P_VS)
    vs_ref[...] = pa.astype(BF16)
    vw_ref[...] = pb.astype(BF16)
    pa, pb = pair(P_KVC)
    kvc_ref[:, :LANES] = pa
    kvc_ref[:, LANES:] = pb
    pg = _dot(h, w_ref[:, P_GATE:P_GATE + LANES])
    gate_ref[...] = jax.nn.sigmoid(pg)


def _proj_src_columns():
    src = np.full((P_TOTAL,), -1, np.int64)
    gidx = np.full((P_NORM,), -1, np.int64)
    d = np.arange(HEAD_DIM)
    for hh in range(N_HEADS_A):
        g = hh // (N_HEADS_A // N_KV_A)
        src[P_QA + hh * LANES + g * HEAD_DIM + d] = hh * HEAD_DIM + d
        gidx[P_QA + hh * LANES + g * HEAD_DIM + d] = d
    for hh in range(N_HEADS_B):
        g = hh // (N_HEADS_B // N_KV_B)
        src[P_QB + hh * LANES + g * HEAD_DIM + d] = OFF_QB + hh * HEAD_DIM + d
        gidx[P_QB + hh * LANES + g * HEAD_DIM + d] = HEAD_DIM + d
    two = np.arange(2 * HEAD_DIM)
    src[P_KA + two] = OFF_KA + two
    gidx[P_KA + two] = 2 * HEAD_DIM + two % HEAD_DIM
    src[P_VA + two] = OFF_VA + two
    for slot, dst in ((0, P_KVC), (1, P_KVC + LANES), (2, P_KS), (3, P_VS), (4, P_KW), (5, P_VW)):
        src[dst + two] = OFF_KVB + slot * KVB + two
    gidx[P_KS + two] = 3 * HEAD_DIM + two % HEAD_DIM
    gidx[P_KW + two] = 3 * HEAD_DIM + two % HEAD_DIM
    src[P_GATE + np.arange(N_GATES)] = OFF_GB + np.arange(N_GATES)
    return src, gidx


def _proj(x2d, nrm, w_mix_in, q_norm_a, k_norm_a, q_norm_b, k_norm_b):
    m = x2d.shape[0]
    tm = FFN_TM
    src, gidx = _proj_src_columns()
    w = jnp.where(jnp.asarray(src >= 0)[None, :], w_mix_in[:, np.maximum(src, 0)], 0.0).astype(BF16)
    scale = HEAD_DIM ** -0.5
    gsrc = jnp.concatenate([q_norm_a * scale, q_norm_b * scale, k_norm_a, k_norm_b]).astype(F32)
    gain = jnp.where(jnp.asarray(gidx >= 0), gsrc[np.maximum(gidx, 0)], 0.0).reshape(1, P_NORM)
    row = lambda i: (i, 0)
    const = lambda i: (0, 0)
    widths = [(N_HEADS_A * LANES, BF16), (N_HEADS_B * LANES, BF16)] + [(LANES, BF16)] * 6 + \
             [(2 * LANES, F32), (LANES, F32)]
    return pl.pallas_call(
        _proj_body,
        out_shape=[jax.ShapeDtypeStruct((m, wd), dt) for wd, dt in widths],
        grid=(m // tm,),
        in_specs=[pl.BlockSpec((tm, D_MODEL), row), pl.BlockSpec((1, D_MODEL), const),
                  pl.BlockSpec((D_MODEL, P_TOTAL), const), pl.BlockSpec((1, P_NORM), const)],
        out_specs=[pl.BlockSpec((tm, wd), row) for wd, _ in widths],
        compiler_params=pltpu.CompilerParams(
            dimension_semantics=("arbitrary",), vmem_limit_bytes=VMEM_LIMIT),
        name="proj",
    )(x2d, nrm.reshape(1, D_MODEL), w, gain)


CMP_FEAT = CMP_STRIDE * 2 * LANES
CMP_SLOTS = 4


def _compress_body(x_ref, pos_ref, w1_ref, b1_ref, w2_ref, b2_ref, gain_ref, kc_ref, vc_ref):
    nck = x_ref.shape[1]
    x = x_ref[0]
    a = _dot((x + pos_ref[0]).astype(BF16), w1_ref[0])
    b = _dot((x + pos_ref[1]).astype(BF16), w1_ref[1])
    pre = a + pltpu.roll(b, nck - 1, 0) + b1_ref[...]
    hid = (pre * jax.nn.sigmoid(pre)).astype(BF16)
    out = _dot(hid, w2_ref[...]) + b2_ref[...]
    k = out[:, :LANES]
    lo = lax.broadcasted_iota(jnp.int32, (nck, LANES), 1) < HEAD_DIM
    sq = k * k
    ms0 = jnp.sum(jnp.where(lo, sq, 0.0), axis=-1, keepdims=True) * (1.0 / HEAD_DIM)
    ms1 = jnp.sum(jnp.where(lo, 0.0, sq), axis=-1, keepdims=True) * (1.0 / HEAD_DIM)
    r = jnp.where(lo, lax.rsqrt(ms0 + EPS), lax.rsqrt(ms1 + EPS))
    kc_ref[0] = (k * r * gain_ref[...]).astype(BF16)
    vc_ref[0] = out[:, LANES:].astype(BF16)


def _compress(kvc, cmp_pos, cmp_w1, cmp_b1, cmp_w2, cmp_b2, k_norm_b):
    b, s, _ = kvc.shape
    nck = s // CMP_STRIDE
    x = kvc.reshape(b, nck, CMP_FEAT)
    r = CMP_LEN // CMP_STRIDE
    w1r = cmp_w1.reshape(2, r, CMP_STRIDE, HEAD_DIM, CMP_HIDDEN)
    eye = jnp.eye(CMP_SLOTS, dtype=F32)
    kv_of_slot = np.array([0, 0, 1, 1])
    w1s = w1r[kv_of_slot]
    w1big = jnp.einsum('sjpdh,st->jpsdth', w1s, eye).reshape(r, CMP_FEAT, CMP_SLOTS * CMP_HIDDEN)
    posr = cmp_pos.reshape(2, r, CMP_STRIDE, HEAD_DIM)[kv_of_slot]
    posbig = posr.transpose(1, 2, 0, 3).reshape(r, 1, CMP_FEAT)
    b1big = cmp_b1[kv_of_slot].reshape(1, CMP_SLOTS * CMP_HIDDEN)
    w2big = jnp.einsum('shd,st->shtd', cmp_w2[kv_of_slot], eye).reshape(
        CMP_SLOTS * CMP_HIDDEN, CMP_SLOTS * HEAD_DIM)
    b2big = cmp_b2[kv_of_slot].reshape(1, CMP_SLOTS * HEAD_DIM)
    gain = jnp.tile(k_norm_b, 2).reshape(1, LANES)
    assert r == 2
    c3 = lambda i: (0, 0, 0)
    c2 = lambda i: (0, 0)
    return pl.pallas_call(
        _compress_body,
        out_shape=[jax.ShapeDtypeStruct((b, nck, LANES), BF16)] * 2,
        grid=(b,),
        in_specs=[pl.BlockSpec((1, nck, CMP_FEAT), lambda i: (i, 0, 0)),
                  pl.BlockSpec((r, 1, CMP_FEAT), c3),
                  pl.BlockSpec((r, CMP_FEAT, CMP_SLOTS * CMP_HIDDEN), c3),
                  pl.BlockSpec((1, CMP_SLOTS * CMP_HIDDEN), c2),
                  pl.BlockSpec((CMP_SLOTS * CMP_HIDDEN, CMP_SLOTS * HEAD_DIM), c2),
                  pl.BlockSpec((1, CMP_SLOTS * HEAD_DIM), c2),
                  pl.BlockSpec((1, LANES), c2)],
        out_specs=[pl.BlockSpec((1, nck, LANES), lambda i: (i, 0, 0))] * 2,
        compiler_params=pltpu.CompilerParams(
            dimension_semantics=("arbitrary",), vmem_limit_bytes=VMEM_LIMIT),
        name="compress",
    )(x, posbig, w1big.astype(BF16), b1big, w2big.astype(BF16), b2big, gain)


def _stack_heads(q_ref, g, heads_per_group):
    return jnp.concatenate(
        [q_ref[0, :, (g * heads_per_group + r) * LANES:(g * heads_per_group + r + 1) * LANES]
         for r in range(heads_per_group)], axis=0)


def _merge_heads(o_heads, g, lane):
    tiles = []
    for c in range(len(o_heads) // 2):
        lo_src = o_heads[2 * c] if g == 0 else pltpu.roll(o_heads[2 * c], HEAD_DIM, 1)
        hi_src = o_heads[2 * c + 1] if g == 1 else pltpu.roll(o_heads[2 * c + 1], HEAD_DIM, 1)
        tiles.append(jnp.where(lane < HEAD_DIM, lo_src, hi_src))
    return tiles


def _band_tiles(tbl):
    i = np.arange(TQ)[:, None]
    j = np.arange(TQ)[None, :]
    prev = tbl[_bucket_np(i - j + TQ)]
    diag = tbl[_bucket_np(i - j)]
    return prev.transpose(2, 0, 1).astype(F32), diag.transpose(2, 0, 1).astype(F32)


def _swa_body(sink_ref, q_ref, k_ref, v_ref, bias_ref, o_ref):
    qt = pl.program_id(1)
    t0 = qt * TQ
    s_prev = pl.multiple_of(jnp.maximum(t0 - TQ, 0), TQ)
    s_diag = pl.multiple_of(t0, TQ)
    k2 = jnp.concatenate([k_ref[0, pl.ds(s_prev, TQ), :], k_ref[0, pl.ds(s_diag, TQ), :]], axis=0)
    v2 = jnp.concatenate([v_ref[0, pl.ds(s_prev, TQ), :], v_ref[0, pl.ds(s_diag, TQ), :]], axis=0)
    ii = lax.broadcasted_iota(jnp.int32, (TQ, 2 * TQ), 0)
    jj = lax.broadcasted_iota(jnp.int32, (TQ, 2 * TQ), 1)
    mask = ((jj < TQ) & (jj > ii) & (qt > 0)) | ((jj >= TQ) & ((jj - TQ) <= ii))
    lane = lax.broadcasted_iota(jnp.int32, (TQ, LANES), 1)
    rpg = N_HEADS_A // N_KV_A
    for g in range(N_KV_A):
        q4 = _stack_heads(q_ref, g, rpg)
        s_all = _nt_dot(q4, k2)
        ps, dens = [], []
        for r in range(rpg):
            hh = g * rpg + r
            s = jnp.where(mask, s_all[r * TQ:(r + 1) * TQ] + bias_ref[hh], NEG)
            sink = sink_ref[hh]
            m = jnp.maximum(jnp.max(s, axis=-1, keepdims=True), sink)
            p = jnp.exp(s - m)
            dens.append(jnp.sum(p, axis=-1, keepdims=True) + jnp.exp(sink - m))
            ps.append(p.astype(BF16))
        o_all = _dot(jnp.concatenate(ps, axis=0), v2)
        o_heads = [o_all[r * TQ:(r + 1) * TQ] / dens[r] for r in range(rpg)]
        for c, tile in enumerate(_merge_heads(o_heads, g, lane)):
            col = (g * rpg // 2 + c) * LANES
            o_ref[0, :, col:col + LANES] = tile.astype(BF16)


def _swa(qa, ka, va, bias, sinks):
    b, s, _ = qa.shape
    assert WINDOW_A == TQ
    return pl.pallas_call(
        _swa_body,
        out_shape=jax.ShapeDtypeStruct((b, s, QA), BF16),
        grid=(b, s // TQ),
        in_specs=[pl.BlockSpec(memory_space=pltpu.SMEM),
                  pl.BlockSpec((1, TQ, N_HEADS_A * LANES), lambda i, j: (i, j, 0)),
                  pl.BlockSpec((1, s, LANES), lambda i, j: (i, 0, 0)),
                  pl.BlockSpec((1, s, LANES), lambda i, j: (i, 0, 0)),
                  pl.BlockSpec((N_HEADS_A, TQ, 2 * TQ), lambda i, j: (0, 0, 0))],
        out_specs=pl.BlockSpec((1, TQ, QA), lambda i, j: (i, j, 0)),
        compiler_params=pltpu.CompilerParams(
            dimension_semantics=("arbitrary", "arbitrary"), vmem_limit_bytes=VMEM_LIMIT),
        name="swa",
    )(sinks.astype(F32), qa, ka, va, bias)


CMP_NEAR = 16
CMP_NEAR_LO = 9


def _nsa_body(q_ref, kc_ref, vc_ref, ks_ref, vs_ref, kw_ref, vw_ref, oh_ref, gate_ref,
              bprev_ref, bdiag_ref, g2_ref, ovl_ref, o_ref, m_ref, acc_ref,
              s0_ref, s1_ref, mc0_ref, mc1_ref, *, n_slc, n_top):
    qt = pl.program_id(1)
    t0 = qt * TQ
    rpg = N_HEADS_B // N_KV_B
    rows = rpg * TQ
    ncmp = kc_ref.shape[1]
    nwb = WINDOW_B // TQ
    s_prev = pl.multiple_of(jnp.maximum(t0 - TQ, 0), TQ)
    s_diag = pl.multiple_of(t0, TQ)
    lane = lax.broadcasted_iota(jnp.int32, (TQ, LANES), 1)
    rowi = lax.broadcasted_iota(jnp.int32, (TQ, LANES), 0)
    causal = lane <= rowi
    ones_col = (lane == 0).astype(BF16)
    has_prev = qt > 0

    c_idx = lax.broadcasted_iota(jnp.int32, (ncmp, LANES), 0)
    l_idx = lax.broadcasted_iota(jnp.int32, (ncmp, LANES), 1)
    place = ((c_idx - (TQ // CMP_STRIDE) * qt + CMP_NEAR_LO) == (l_idx & (CMP_NEAR - 1))) \
        & (l_idx < 2 * CMP_NEAR)
    kc_aug = jnp.concatenate([kc_ref[0], place.astype(BF16)], axis=1)
    qi = lax.broadcasted_iota(jnp.int32, (TQ, ncmp), 0)
    ci = lax.broadcasted_iota(jnp.int32, (TQ, ncmp), 1)
    mask_c = (t0 + qi - CMP_STRIDE * ci - (CMP_LEN - 1)) >= 0
    vc = vc_ref[0]

    cur = (t0 + rowi) // SEL_LEN
    valid = (lane <= cur) & (lane < n_slc)
    forced = (lane == 0) | (lane == cur) | (lane == cur - 1)
    n_io = lax.broadcasted_iota(jnp.int32, (n_slc, TQ), 0)
    cur0 = t0 // SEL_LEN
    near_lo = cur0 - TQ // SEL_LEN

    ks2 = jnp.concatenate(
        [jnp.concatenate([ks_ref[0, pl.ds(s_prev, TQ), :], oh_ref[pl.ds(s_prev, TQ), :]], axis=1),
         jnp.concatenate([ks_ref[0, pl.ds(s_diag, TQ), :], oh_ref[pl.ds(s_diag, TQ), :]], axis=1)],
        axis=0)
    vs2 = jnp.concatenate(
        [jnp.concatenate([vs_ref[0, pl.ds(s_prev, TQ), :], ones_col], axis=1),
         jnp.concatenate([vs_ref[0, pl.ds(s_diag, TQ), :], ones_col], axis=1)], axis=0)
    ones_far = jnp.concatenate([ones_col] * (FAR_TK // TQ), axis=0)

    w_starts = [pl.multiple_of(jnp.maximum(t0 - kb * TQ, 0), TQ) for kb in range(nwb, -1, -1)]
    kw5 = jnp.concatenate([kw_ref[0, pl.ds(st, TQ), :] for st in w_starts], axis=0)
    vw5 = jnp.concatenate([vw_ref[0, pl.ds(st, TQ), :] for st in w_starts], axis=0)

    q4s, o_cmps, q_fars = [], [], []
    for g in range(N_KV_B):
        q4 = _stack_heads(q_ref, g, rpg)
        q4s.append(q4)

        lc_all = _nt_dot(jnp.concatenate([q4, g2_ref[g]], axis=1), kc_aug)
        pcs = None
        pc_rows = []
        for r in range(rpg):
            lc = jnp.where(mask_c, lc_all[r * TQ:(r + 1) * TQ], NEG)
            e = jnp.where(mask_c, jnp.exp(lc - jnp.max(lc, axis=-1, keepdims=True)), 0.0)
            den = jnp.maximum(jnp.sum(e, axis=-1, keepdims=True), jnp.finfo(F32).tiny)
            pc = e / den
            pcs = pc if pcs is None else pcs + pc
            pc_rows.append(pc.astype(BF16))
        o_cmps.append(_dot(jnp.concatenate(pc_rows, axis=0), vc))

        pcs_hi = pcs.astype(BF16)
        pcs_lo = (pcs - pcs_hi.astype(F32)).astype(BF16)
        imp = _dot(pcs_hi, ovl_ref[...]) + _dot(pcs_lo, ovl_ref[...])
        score = jnp.where(valid, imp + jnp.where(forced, FORCE_BONUS, 0.0), NEG)
        s_t = score.T[:n_slc]
        rank = jnp.zeros((n_slc, TQ), F32)
        for mm in range(n_slc):
            sm = s_t[mm:mm + 1, :]
            beats = (sm > s_t) | ((sm == s_t) & (n_io > mm))
            rank = rank + jnp.where(beats, 1.0, 0.0)
        sel_t = jnp.where((rank < n_top) & (s_t > 0.5 * NEG), 1.0, 0.0)
        if n_slc < LANES:
            sel_t = jnp.concatenate([sel_t, jnp.zeros((LANES - n_slc, TQ), F32)], axis=0)
        sel = sel_t.T > 0.5
        m_near = jnp.where(sel & (lane >= near_lo), 0.0, NEG).astype(BF16)
        m_far = jnp.where(sel & (lane < near_lo), 0.0, NEG).astype(BF16)
        q_near = jnp.concatenate([q4, jnp.concatenate([m_near] * rpg, axis=0)], axis=1)
        q_fars.append(jnp.concatenate([q4, jnp.concatenate([m_far] * rpg, axis=0)], axis=1))

        s_all = _nt_dot(q_near, ks2)
        s_rows = []
        for r in range(rpg):
            hh = g * rpg + r
            sr = s_all[r * TQ:(r + 1) * TQ]
            s_p = jnp.where(has_prev, sr[:, :TQ] + bprev_ref[hh], NEG)
            s_d = jnp.where(causal, sr[:, TQ:] + bdiag_ref[hh], NEG)
            s_rows.append(jnp.concatenate([s_p, s_d], axis=1))
        s_near = jnp.concatenate(s_rows, axis=0)
        m0 = jnp.max(s_near, axis=-1, keepdims=True)
        m_ref[g * rows:(g + 1) * rows, :] = m0
        acc_ref[g * rows:(g + 1) * rows, :] = _dot(jnp.exp(s_near - m0).astype(BF16), vs2)

    q_far = jnp.concatenate(q_fars, axis=0)
    n_far = (qt + FAR_TK // TQ - 2) // (FAR_TK // TQ)
    last_chunk = ks_ref.shape[1] // FAR_TK - 1
    s_bufs = (s0_ref, s1_ref)
    mc_bufs = (mc0_ref, mc1_ref)

    def chunk_start(c):
        return pl.multiple_of(jnp.minimum(c, last_chunk) * FAR_TK, FAR_TK)

    def produce(slot, c):
        k0 = chunk_start(c)
        kk = jnp.concatenate([ks_ref[0, pl.ds(k0, FAR_TK), :], oh_ref[pl.ds(k0, FAR_TK), :]], axis=1)
        s = _nt_dot(q_far, kk)
        s_bufs[slot][...] = s
        mc_bufs[slot][...] = jnp.max(s, axis=-1, keepdims=True)

    def consume(slot, c):
        k0 = chunk_start(c)
        vv = jnp.concatenate([vs_ref[0, pl.ds(k0, FAR_TK), :], ones_far], axis=1)
        m_old = m_ref[...]
        m_new = jnp.maximum(m_old, mc_bufs[slot][...])
        alpha = jnp.exp(m_old - m_new)
        p = jnp.exp(s_bufs[slot][...] - m_new).astype(BF16)
        acc_ref[...] = acc_ref[...] * alpha + _dot(p, vv)
        m_ref[...] = m_new

    produce(0, 0)

    def far_pair(j, carry):
        produce(1, 2 * j + 1)
        consume(0, 2 * j)
        produce(0, 2 * j + 2)
        consume(1, 2 * j + 1)
        return carry

    lax.fori_loop(0, (n_far + 1) // 2, far_pair, 0)
    acc = acc_ref[...]
    o_sel_all = acc[:, :LANES] / acc[:, LANES:LANES + 1]

    for g in range(N_KV_B):
        q4 = q4s[g]
        o_cmp = o_cmps[g]
        o_sel = o_sel_all[g * rows:(g + 1) * rows]

        w_all = _nt_dot(q4, kw5)
        pw, lw = [], []
        for r in range(rpg):
            hh = g * rpg + r
            wr = w_all[r * TQ:(r + 1) * TQ]
            blocks = []
            for bi, kb in enumerate(range(nwb, -1, -1)):
                blk = wr[:, bi * TQ:(bi + 1) * TQ]
                if kb == nwb:
                    blk = jnp.where((lane > rowi) & (qt >= kb), blk, NEG)
                elif kb >= 2:
                    blk = jnp.where(qt >= kb, blk, NEG)
                elif kb == 1:
                    blk = jnp.where(has_prev, blk + bprev_ref[hh], NEG)
                else:
                    blk = jnp.where(causal, blk + bdiag_ref[hh], NEG)
                blocks.append(blk)
            sw = jnp.concatenate(blocks, axis=1)
            p = jnp.exp(sw - jnp.max(sw, axis=-1, keepdims=True))
            lw.append(jnp.sum(p, axis=-1, keepdims=True))
            pw.append(p.astype(BF16))
        o_win = _dot(jnp.concatenate(pw, axis=0), vw5)

        o_heads = []
        for r in range(rpg):
            hh = g * rpg + r
            sl = slice(r * TQ, (r + 1) * TQ)
            g_cmp = gate_ref[0, :, hh:hh + 1]
            g_sel = gate_ref[0, :, N_HEADS_B + hh:N_HEADS_B + hh + 1]
            g_win = gate_ref[0, :, 2 * N_HEADS_B + hh:2 * N_HEADS_B + hh + 1]
            o_heads.append(g_cmp * o_cmp[sl] + g_sel * o_sel[sl] + g_win * (o_win[sl] / lw[r]))
        for c, tile in enumerate(_merge_heads(o_heads, g, lane)):
            col = (g * rpg // 2 + c) * LANES
            o_ref[0, :, col:col + LANES] = tile.astype(BF16)


def _nsa(qb, kc, vc, ks, vs, kw, vw, gates, bprev, bdiag, g2):
    b, s, _ = qb.shape
    assert s % (2 * FAR_TK) == 0 and WINDOW_B % TQ == 0 and TQ % SEL_LEN == 0
    n_slc = s // SEL_LEN
    assert n_slc <= LANES and n_slc % 8 == 0
    n_top = min(SEL_TOPK, n_slc)
    ncmp = s // CMP_STRIDE
    rpg = N_HEADS_B // N_KV_B
    kpos = np.arange(s)
    onehot = (kpos[:, None] // SEL_LEN == np.arange(LANES)[None, :]).astype(np.float32)
    cstart = np.arange(ncmp) * CMP_STRIDE
    sstart = np.arange(LANES) * SEL_LEN
    ovl = ((cstart[:, None] < sstart[None, :] + SEL_LEN)
           & (cstart[:, None] + CMP_LEN - 1 >= sstart[None, :])
           & (np.arange(ncmp)[:, None] < ncmp - CMP_LEN // CMP_STRIDE + 1)
           & (np.arange(LANES)[None, :] < n_slc)).astype(np.float32)
    full = lambda i, j: (i, 0, 0)
    return pl.pallas_call(
        functools.partial(_nsa_body, n_slc=n_slc, n_top=n_top),
        out_shape=jax.ShapeDtypeStruct((b, s, QB), BF16),
        grid=(b, s // TQ),
        in_specs=[pl.BlockSpec((1, TQ, N_HEADS_B * LANES), lambda i, j: (i, j, 0)),
                  pl.BlockSpec((1, ncmp, LANES), full), pl.BlockSpec((1, ncmp, LANES), full),
                  pl.BlockSpec((1, s, LANES), full), pl.BlockSpec((1, s, LANES), full),
                  pl.BlockSpec((1, s, LANES), full), pl.BlockSpec((1, s, LANES), full),
                  pl.BlockSpec((s, LANES), lambda i, j: (0, 0)),
                  pl.BlockSpec((1, TQ, LANES), lambda i, j: (i, j, 0)),
                  pl.BlockSpec((N_HEADS_B, TQ, TQ), lambda i, j: (0, 0, 0)),
                  pl.BlockSpec((N_HEADS_B, TQ, TQ), lambda i, j: (0, 0, 0)),
                  pl.BlockSpec((N_KV_B, rpg * TQ, LANES), lambda i, j: (0, 0, 0)),
                  pl.BlockSpec((ncmp, LANES), lambda i, j: (0, 0))],
        out_specs=pl.BlockSpec((1, TQ, QB), lambda i, j: (i, j, 0)),
        scratch_shapes=[pltpu.VMEM((N_KV_B * rpg * TQ, 1), F32),
                        pltpu.VMEM((N_KV_B * rpg * TQ, 2 * LANES), F32),
                        pltpu.VMEM((N_KV_B * rpg * TQ, FAR_TK), F32),
                        pltpu.VMEM((N_KV_B * rpg * TQ, FAR_TK), F32),
                        pltpu.VMEM((N_KV_B * rpg * TQ, 1), F32),
                        pltpu.VMEM((N_KV_B * rpg * TQ, 1), F32)],
        compiler_params=pltpu.CompilerParams(
            dimension_semantics=("arbitrary", "arbitrary"), vmem_limit_bytes=VMEM_LIMIT),
        name="nsa",
    )(qb, kc, vc, ks, vs, kw, vw, jnp.asarray(onehot, BF16), gates, bprev, bdiag, g2,
      jnp.asarray(ovl, BF16))


def _cmp_bias_operand(tbl_b):
    i = np.arange(TQ)[:, None]
    u = np.arange(CMP_NEAR)[None, :]
    rel = i - CMP_STRIDE * (u - CMP_NEAR_LO) - (CMP_LEN - 1)
    assert rel[0, 0] + CMP_STRIDE >= MAX_DISTANCE and rel[-1, -1] - CMP_STRIDE < 0
    near = (rel >= 0) & (rel < MAX_DISTANCE)
    far_const = tbl_b[NUM_BUCKETS - 1]
    vals = jnp.where(jnp.asarray(near)[:, :, None], tbl_b[_bucket_np(rel)] - far_const, 0.0)
    vals = vals.transpose(2, 0, 1).astype(F32)
    hi = vals.astype(BF16)
    lo = (vals - hi.astype(F32)).astype(BF16)
    pad = jnp.zeros((N_HEADS_B, TQ, LANES - 2 * CMP_NEAR), BF16)
    out = jnp.concatenate([hi, lo, pad], axis=-1)
    rpg = N_HEADS_B // N_KV_B
    return out.reshape(N_KV_B, rpg * TQ, LANES)


def _layer(x, tbl, ffn1_norm, ffn1_w_in, ffn1_w_out, mix_norm, w_mix_in, w_mix_out,
           q_norm_a, k_norm_a, sinks_a, q_norm_b, k_norm_b,
           cmp_pos, cmp_w1, cmp_b1, cmp_w2, cmp_b2, ffn2_norm, ffn2_w_in, ffn2_w_out):
    b, s, d = x.shape
    x2d = x.reshape(b * s, d)
    x1 = _ffn(x2d, ffn1_norm, ffn1_w_in.astype(BF16), ffn1_w_out.astype(BF16))
    qa, qb, ka, ks, kw, va, vs, vw, kvc, gates = _proj(
        x1, mix_norm, w_mix_in, q_norm_a, k_norm_a, q_norm_b, k_norm_b)
    r3 = lambda a: a.reshape(b, s, a.shape[-1])
    kc, vc = _compress(r3(kvc), cmp_pos, cmp_w1, cmp_b1, cmp_w2, cmp_b2, k_norm_b)

    prev_t, diag_t = _band_tiles(tbl)
    bias_a = jnp.concatenate([prev_t[:N_HEADS_A], diag_t[:N_HEADS_A]], axis=-1)
    far_b = tbl[NUM_BUCKETS - 1, N_HEADS_A:].astype(F32)[:, None, None]
    bprev_b = prev_t[N_HEADS_A:] - far_b
    bdiag_b = diag_t[N_HEADS_A:] - far_b
    g2 = _cmp_bias_operand(tbl[:, N_HEADS_A:])

    mix_a = _swa(r3(qa), r3(ka), r3(va), bias_a, sinks_a)
    mix_b = _nsa(r3(qb), kc, vc, r3(ks), r3(vs), r3(kw), r3(vw), r3(gates), bprev_b, bdiag_b, g2)
    y = _ffn(x1, ffn2_norm, ffn2_w_in.astype(BF16), ffn2_w_out.astype(BF16),
             pre=(mix_a.reshape(b * s, QA), mix_b.reshape(b * s, QB), w_mix_out.astype(BF16)))
    return y.reshape(b, s, d)


def kernel(x, rel_bias_table, ffn1_norm, ffn1_w_in, ffn1_w_out, mix_norm, w_mix_in, w_mix_out,
           q_norm_a, k_norm_a, sinks_a, q_norm_b, k_norm_b,
           cmp_pos, cmp_w1, cmp_b1, cmp_w2, cmp_b2, ffn2_norm, ffn2_w_in, ffn2_w_out):
    for l in range(ffn1_norm.shape[0]):
        x = _layer(x, rel_bias_table, ffn1_norm[l], ffn1_w_in[l], ffn1_w_out[l],
                   mix_norm[l], w_mix_in[l], w_mix_out[l],
                   q_norm_a[l], k_norm_a[l], sinks_a[l], q_norm_b[l], k_norm_b[l],
                   cmp_pos[l], cmp_w1[l], cmp_b1[l], cmp_w2[l], cmp_b2[l],
                   ffn2_norm[l], ffn2_w_in[l], ffn2_w_out[l])
    return x
```

```python
import functools
import math

import numpy as np
import jax
import jax.numpy as jnp
from jax import lax
from jax.experimental import pallas as pl
from jax.experimental.pallas import tpu as pltpu

F32 = jnp.float32
BF16 = jnp.bfloat16

D_MODEL = 1024
HEAD_DIM = 64
N_HEADS_A = 8
N_KV_A = 2
N_HEADS_B = 8
N_KV_B = 2
N_HEADS = N_HEADS_A + N_HEADS_B
WINDOW_A = 128
CMP_LEN = 32
CMP_STRIDE = 16
CMP_HIDDEN = 128
SEL_LEN = 64
SEL_TOPK = 16
WINDOW_B = 512
FORCE_BONUS = 1000.0
NUM_BUCKETS = 32
MAX_DISTANCE = 128
D_FF = 2816
EPS = 1e-6
NEG = -1e9

QA = N_HEADS_A * HEAD_DIM
KVA = N_KV_A * HEAD_DIM
QB = N_HEADS_B * HEAD_DIM
KVB = N_KV_B * HEAD_DIM
N_GATES = 3 * N_HEADS_B
OFF_KA = QA
OFF_VA = OFF_KA + KVA
OFF_QB = OFF_VA + KVA
OFF_KVB = OFF_QB + QB
OFF_GB = OFF_KVB + 6 * KVB
D_IN = OFF_GB + N_GATES

LANES = 128
TQ = 128
FAR_TK = 512
FFN_TM = 512
FFN_TF = 256
VMEM_LIMIT = 56 * 1024 * 1024

P_QA = 0
P_QB = P_QA + N_HEADS_A * LANES
P_KA = P_QB + N_HEADS_B * LANES
P_KS = P_KA + LANES
P_KW = P_KS + LANES
P_VA = P_KW + LANES
P_VS = P_VA + LANES
P_VW = P_VS + LANES
P_KVC = P_VW + LANES
P_GATE = P_KVC + 2 * LANES
P_TOTAL = P_GATE + LANES
P_NORM = P_VA


def _nt_dot(a, b):
    return lax.dot_general(a, b, (((1,), (1,)), ((), ())), preferred_element_type=F32)


def _dot(a, b):
    return jnp.dot(a, b, preferred_element_type=F32)


def _rms_rows(x, g):
    ms = jnp.mean(x * x, axis=-1, keepdims=True)
    return x * lax.rsqrt(ms + EPS) * g


def _bucket_np(rel):
    n = np.maximum(rel, 0)
    max_exact = NUM_BUCKETS // 2
    nf = np.maximum(n, 1).astype(np.float64)
    large = max_exact + (np.log(nf / max_exact) / math.log(MAX_DISTANCE / max_exact)
                         * (NUM_BUCKETS - max_exact)).astype(np.int64)
    large = np.minimum(large, NUM_BUCKETS - 1)
    return np.where(n < max_exact, n, large).astype(np.int32)


def _ffn_body(*refs, pre):
    if pre:
        x_ref, ma_ref, mb_ref, wo_ref, nrm_ref, win_ref, wout_ref, o_ref, acc_ref = refs
    else:
        x_ref, nrm_ref, win_ref, wout_ref, o_ref, acc_ref = refs
    x = x_ref[...]
    if pre:
        x = x + _dot(ma_ref[...], wo_ref[0:QA, :]) + _dot(mb_ref[...], wo_ref[QA:QA + QB, :])
    h = _rms_rows(x, nrm_ref[...]).astype(BF16)
    for c in range(D_FF // FFN_TF):
        gate = _dot(h, win_ref[:, c * FFN_TF:(c + 1) * FFN_TF])
        up = _dot(h, win_ref[:, D_FF + c * FFN_TF:D_FF + (c + 1) * FFN_TF])
        a = (gate * jax.nn.sigmoid(gate) * up).astype(BF16)
        part = _dot(a, wout_ref[c * FFN_TF:(c + 1) * FFN_TF, :])
        if c == 0:
            acc_ref[...] = part
        else:
            acc_ref[...] += part
    o_ref[...] = x + 0.5 * acc_ref[...]


def _ffn(x2d, nrm, w_in, w_out, pre=None):
    m = x2d.shape[0]
    tm = FFN_TM
    assert m % tm == 0
    row = lambda i: (i, 0)
    const = lambda i: (0, 0)
    in_specs = [pl.BlockSpec((tm, D_MODEL), row)]
    args = [x2d]
    if pre is not None:
        mix_a, mix_b, wo = pre
        in_specs += [pl.BlockSpec((tm, QA), row), pl.BlockSpec((tm, QB), row),
                     pl.BlockSpec((QA + QB, D_MODEL), const)]
        args += [mix_a, mix_b, wo]
    in_specs += [pl.BlockSpec((1, D_MODEL), const),
                 pl.BlockSpec((D_MODEL, 2 * D_FF), const),
                 pl.BlockSpec((D_FF, D_MODEL), const)]
    args += [nrm.reshape(1, D_MODEL), w_in, w_out]
    return pl.pallas_call(
        functools.partial(_ffn_body, pre=pre is not None),
        out_shape=jax.ShapeDtypeStruct((m, D_MODEL), F32),
        grid=(m // tm,),
        in_specs=in_specs,
        out_specs=pl.BlockSpec((tm, D_MODEL), row),
        scratch_shapes=[pltpu.VMEM((tm, D_MODEL), F32)],
        compiler_params=pltpu.CompilerParams(
            dimension_semantics=("arbitrary",), vmem_limit_bytes=VMEM_LIMIT),
        name="ffn_pre" if pre is not None else "ffn",
    )(*args)


def _proj_body(x_ref, nrm_ref, w_ref, gain_ref, qa_ref, qb_ref, ka_ref, ks_ref, kw_ref,
               va_ref, vs_ref, vw_ref, kvc_ref, gate_ref):
    tm = x_ref.shape[0]
    h = _rms_rows(x_ref[...], nrm_ref[...]).astype(BF16)
    lo = lax.broadcasted_iota(jnp.int32, (tm, LANES), 1) < HEAD_DIM
    inv_d = 1.0 / HEAD_DIM

    def pair(c0):
        p2 = _dot(h, w_ref[:, c0:c0 + 2 * LANES])
        return p2[:, :LANES], p2[:, LANES:]

    def one_head(p, c0):
        ms = jnp.sum(p * p, axis=-1, keepdims=True) * inv_d
        return (p * lax.rsqrt(ms + EPS) * gain_ref[:, c0:c0 + LANES]).astype(BF16)

    def two_heads(p, c0):
        sq = p * p
        ms0 = jnp.sum(jnp.where(lo, sq, 0.0), axis=-1, keepdims=True) * inv_d
        ms1 = jnp.sum(jnp.where(lo, 0.0, sq), axis=-1, keepdims=True) * inv_d
        r = jnp.where(lo, lax.rsqrt(ms0 + EPS), lax.rsqrt(ms1 + EPS))
        return (p * r * gain_ref[:, c0:c0 + LANES]).astype(BF16)

    for hp in range(N_HEADS // 2):
        c0 = 2 * hp * LANES
        pa, pb = pair(c0)
        ref = qa_ref if c0 < P_QB else qb_ref
        l0 = c0 - (P_QA if c0 < P_QB else P_QB)
        ref[:, l0:l0 + LANES] = one_head(pa, c0)
        ref[:, l0 + LANES:l0 + 2 * LANES] = one_head(pb, c0 + LANES)
    pa, pb = pair(P_KA)
    ka_ref[...] = two_heads(pa, P_KA)
    ks_ref[...] = two_heads(pb, P_KS)
    pa, pb = pair(P_KW)
    kw_ref[...] = two_heads(pa, P_KW)
    va_ref[...] = pb.astype(BF16)
    pa, pb = pair(P_VS)
    vs_ref[...] = pa.astype(BF16)
    vw_ref[...] = pb.astype(BF16)
    pa, pb = pair(P_KVC)
    kvc_ref[:, :LANES] = pa
    kvc_ref[:, LANES:] = pb
    pg = _dot(h, w_ref[:, P_GATE:P_GATE + LANES])
    gate_ref[...] = jax.nn.sigmoid(pg)


def _proj_src_columns():
    src = np.full((P_TOTAL,), -1, np.int64)
    gidx = np.full((P_NORM,), -1, np.int64)
    d = np.arange(HEAD_DIM)
    for hh in range(N_HEADS_A):
        g = hh // (N_HEADS_A // N_KV_A)
        src[P_QA + hh * LANES + g * HEAD_DIM + d] = hh * HEAD_DIM + d
        gidx[P_QA + hh * LANES + g * HEAD_DIM + d] = d
    for hh in range(N_HEADS_B):
        g = hh // (N_HEADS_B // N_KV_B)
        src[P_QB + hh * LANES + g * HEAD_DIM + d] = OFF_QB + hh * HEAD_DIM + d
        gidx[P_QB + hh * LANES + g * HEAD_DIM + d] = HEAD_DIM + d
    two = np.arange(2 * HEAD_DIM)
    src[P_KA + two] = OFF_KA + two
    gidx[P_KA + two] = 2 * HEAD_DIM + two % HEAD_DIM
    src[P_VA + two] = OFF_VA + two
    for slot, dst in ((0, P_KVC), (1, P_KVC + LANES), (2, P_KS), (3, P_VS), (4, P_KW), (5, P_VW)):
        src[dst + two] = OFF_KVB + slot * KVB + two
    gidx[P_KS + two] = 3 * HEAD_DIM + two % HEAD_DIM
    gidx[P_KW + two] = 3 * HEAD_DIM + two % HEAD_DIM
    src[P_GATE + np.arange(N_GATES)] = OFF_GB + np.arange(N_GATES)
    return src, gidx


def _proj(x2d, nrm, w_mix_in, q_norm_a, k_norm_a, q_norm_b, k_norm_b):
    m = x2d.shape[0]
    tm = FFN_TM
    src, gidx = _proj_src_columns()
    pieces, c = [], 0
    while c < P_TOTAL:
        e = c + 1
        while e < P_TOTAL and ((src[e] < 0) == (src[c] < 0)) and (src[c] < 0 or src[e] == src[e - 1] + 1):
            e += 1
        pieces.append(jnp.zeros((D_MODEL, e - c), w_mix_in.dtype) if src[c] < 0
                      else w_mix_in[:, src[c]:src[c] + e - c])
        c = e
    w = jnp.concatenate(pieces, axis=1).astype(BF16)
    scale = HEAD_DIM ** -0.5
    gsrc = jnp.concatenate([q_norm_a * scale, q_norm_b * scale, k_norm_a, k_norm_b]).astype(F32)
    gain = jnp.where(jnp.asarray(gidx >= 0), gsrc[np.maximum(gidx, 0)], 0.0).reshape(1, P_NORM)
    row = lambda i: (i, 0)
    const = lambda i: (0, 0)
    widths = [(N_HEADS_A * LANES, BF16), (N_HEADS_B * LANES, BF16)] + [(LANES, BF16)] * 6 + \
             [(2 * LANES, F32), (LANES, F32)]
    return pl.pallas_call(
        _proj_body,
        out_shape=[jax.ShapeDtypeStruct((m, wd), dt) for wd, dt in widths],
        grid=(m // tm,),
        in_specs=[pl.BlockSpec((tm, D_MODEL), row), pl.BlockSpec((1, D_MODEL), const),
                  pl.BlockSpec((D_MODEL, P_TOTAL), const), pl.BlockSpec((1, P_NORM), const)],
        out_specs=[pl.BlockSpec((tm, wd), row) for wd, _ in widths],
        compiler_params=pltpu.CompilerParams(
            dimension_semantics=("arbitrary",), vmem_limit_bytes=VMEM_LIMIT),
        name="proj",
    )(x2d, nrm.reshape(1, D_MODEL), w, gain)


CMP_FEAT = CMP_STRIDE * 2 * LANES
CMP_SLOTS = 4


def _compress_body(x_ref, pos_ref, w1_ref, b1_ref, w2_ref, b2_ref, gain_ref, kc_ref, vc_ref):
    nck = x_ref.shape[1]
    x = x_ref[0]
    a = _dot((x + pos_ref[0]).astype(BF16), w1_ref[0])
    b = _dot((x + pos_ref[1]).astype(BF16), w1_ref[1])
    pre = a + pltpu.roll(b, nck - 1, 0) + b1_ref[...]
    hid = (pre * jax.nn.sigmoid(pre)).astype(BF16)
    out = _dot(hid, w2_ref[...]) + b2_ref[...]
    k = out[:, :LANES]
    lo = lax.broadcasted_iota(jnp.int32, (nck, LANES), 1) < HEAD_DIM
    sq = k * k
    ms0 = jnp.sum(jnp.where(lo, sq, 0.0), axis=-1, keepdims=True) * (1.0 / HEAD_DIM)
    ms1 = jnp.sum(jnp.where(lo, 0.0, sq), axis=-1, keepdims=True) * (1.0 / HEAD_DIM)
    r = jnp.where(lo, lax.rsqrt(ms0 + EPS), lax.rsqrt(ms1 + EPS))
    kc_ref[0] = (k * r * gain_ref[...]).astype(BF16)
    vc_ref[0] = out[:, LANES:].astype(BF16)


def _compress(kvc, cmp_pos, cmp_w1, cmp_b1, cmp_w2, cmp_b2, k_norm_b):
    b, s, _ = kvc.shape
    nck = s // CMP_STRIDE
    x = kvc.reshape(b, nck, CMP_FEAT)
    r = CMP_LEN // CMP_STRIDE
    w1r = cmp_w1.reshape(2, r, CMP_STRIDE, HEAD_DIM, CMP_HIDDEN)
    eye = jnp.eye(CMP_SLOTS, dtype=F32)
    kv_of_slot = np.array([0, 0, 1, 1])
    w1s = w1r[kv_of_slot]
    w1big = jnp.einsum('sjpdh,st->jpsdth', w1s, eye).reshape(r, CMP_FEAT, CMP_SLOTS * CMP_HIDDEN)
    posr = cmp_pos.reshape(2, r, CMP_STRIDE, HEAD_DIM)[kv_of_slot]
    posbig = posr.transpose(1, 2, 0, 3).reshape(r, 1, CMP_FEAT)
    b1big = cmp_b1[kv_of_slot].reshape(1, CMP_SLOTS * CMP_HIDDEN)
    w2big = jnp.einsum('shd,st->shtd', cmp_w2[kv_of_slot], eye).reshape(
        CMP_SLOTS * CMP_HIDDEN, CMP_SLOTS * HEAD_DIM)
    b2big = cmp_b2[kv_of_slot].reshape(1, CMP_SLOTS * HEAD_DIM)
    gain = jnp.tile(k_norm_b, 2).reshape(1, LANES)
    assert r == 2
    c3 = lambda i: (0, 0, 0)
    c2 = lambda i: (0, 0)
    return pl.pallas_call(
        _compress_body,
        out_shape=[jax.ShapeDtypeStruct((b, nck, LANES), BF16)] * 2,
        grid=(b,),
        in_specs=[pl.BlockSpec((1, nck, CMP_FEAT), lambda i: (i, 0, 0)),
                  pl.BlockSpec((r, 1, CMP_FEAT), c3),
                  pl.BlockSpec((r, CMP_FEAT, CMP_SLOTS * CMP_HIDDEN), c3),
                  pl.BlockSpec((1, CMP_SLOTS * CMP_HIDDEN), c2),
                  pl.BlockSpec((CMP_SLOTS * CMP_HIDDEN, CMP_SLOTS * HEAD_DIM), c2),
                  pl.BlockSpec((1, CMP_SLOTS * HEAD_DIM), c2),
                  pl.BlockSpec((1, LANES), c2)],
        out_specs=[pl.BlockSpec((1, nck, LANES), lambda i: (i, 0, 0))] * 2,
        compiler_params=pltpu.CompilerParams(
            dimension_semantics=("arbitrary",), vmem_limit_bytes=VMEM_LIMIT),
        name="compress",
    )(x, posbig, w1big.astype(BF16), b1big, w2big.astype(BF16), b2big, gain)


def _stack_heads(q_ref, g, heads_per_group):
    return jnp.concatenate(
        [q_ref[0, :, (g * heads_per_group + r) * LANES:(g * heads_per_group + r + 1) * LANES]
         for r in range(heads_per_group)], axis=0)


def _merge_heads(o_heads, g, lane):
    tiles = []
    for c in range(len(o_heads) // 2):
        lo_src = o_heads[2 * c] if g == 0 else pltpu.roll(o_heads[2 * c], HEAD_DIM, 1)
        hi_src = o_heads[2 * c + 1] if g == 1 else pltpu.roll(o_heads[2 * c + 1], HEAD_DIM, 1)
        tiles.append(jnp.where(lane < HEAD_DIM, lo_src, hi_src))
    return tiles


def _band_tiles(tbl):
    i = np.arange(TQ)[:, None]
    j = np.arange(TQ)[None, :]
    return _table_lookup(tbl, _bucket_np(i - j + TQ)), _table_lookup(tbl, _bucket_np(i - j))


def _table_lookup(tbl, buckets):
    onehot = (buckets.reshape(-1)[:, None] == np.arange(NUM_BUCKETS)[None, :]).astype(np.float32)
    out = jnp.einsum('pk,kh->hp', jnp.asarray(onehot), tbl.astype(F32), precision=lax.Precision.HIGHEST)
    return out.reshape((tbl.shape[1],) + buckets.shape)


def _swa_body(sink_ref, q_ref, k_ref, v_ref, bias_ref, o_ref):
    qt = pl.program_id(1)
    t0 = qt * TQ
    s_prev = pl.multiple_of(jnp.maximum(t0 - TQ, 0), TQ)
    s_diag = pl.multiple_of(t0, TQ)
    k2 = jnp.concatenate([k_ref[0, pl.ds(s_prev, TQ), :], k_ref[0, pl.ds(s_diag, TQ), :]], axis=0)
    v2 = jnp.concatenate([v_ref[0, pl.ds(s_prev, TQ), :], v_ref[0, pl.ds(s_diag, TQ), :]], axis=0)
    ii = lax.broadcasted_iota(jnp.int32, (TQ, 2 * TQ), 0)
    jj = lax.broadcasted_iota(jnp.int32, (TQ, 2 * TQ), 1)
    mask = ((jj < TQ) & (jj > ii) & (qt > 0)) | ((jj >= TQ) & ((jj - TQ) <= ii))
    lane = lax.broadcasted_iota(jnp.int32, (TQ, LANES), 1)
    rpg = N_HEADS_A // N_KV_A
    for g in range(N_KV_A):
        q4 = _stack_heads(q_ref, g, rpg)
        s_all = _nt_dot(q4, k2)
        ps, dens = [], []
        for r in range(rpg):
            hh = g * rpg + r
            s = jnp.where(mask, s_all[r * TQ:(r + 1) * TQ] + bias_ref[hh], NEG)
            sink = sink_ref[hh]
            m = jnp.maximum(jnp.max(s, axis=-1, keepdims=True), sink)
            p = jnp.exp(s - m)
            dens.append(jnp.sum(p, axis=-1, keepdims=True) + jnp.exp(sink - m))
            ps.append(p.astype(BF16))
        o_all = _dot(jnp.concatenate(ps, axis=0), v2)
        o_heads = [o_all[r * TQ:(r + 1) * TQ] / dens[r] for r in range(rpg)]
        for c, tile in enumerate(_merge_heads(o_heads, g, lane)):
            col = (g * rpg // 2 + c) * LANES
            o_ref[0, :, col:col + LANES] = tile.astype(BF16)


def _swa(qa, ka, va, bias, sinks):
    b, s, _ = qa.shape
    assert WINDOW_A == TQ
    return pl.pallas_call(
        _swa_body,
        out_shape=jax.ShapeDtypeStruct((b, s, QA), BF16),
        grid=(b, s // TQ),
        in_specs=[pl.BlockSpec(memory_space=pltpu.SMEM),
                  pl.BlockSpec((1, TQ, N_HEADS_A * LANES), lambda i, j: (i, j, 0)),
                  pl.BlockSpec((1, s, LANES), lambda i, j: (i, 0, 0)),
                  pl.BlockSpec((1, s, LANES), lambda i, j: (i, 0, 0)),
                  pl.BlockSpec((N_HEADS_A, TQ, 2 * TQ), lambda i, j: (0, 0, 0))],
        out_specs=pl.BlockSpec((1, TQ, QA), lambda i, j: (i, j, 0)),
        compiler_params=pltpu.CompilerParams(
            dimension_semantics=("arbitrary", "arbitrary"), vmem_limit_bytes=VMEM_LIMIT),
        name="swa",
    )(sinks.astype(F32), qa, ka, va, bias)


CMP_NEAR = 16
CMP_NEAR_LO = 9


def _nsa_body(q_ref, kc_ref, vc_ref, ks_ref, vs_ref, kw_ref, vw_ref, oh_ref, gate_ref,
              bprev_ref, bdiag_ref, g2_ref, ovl_ref, o_ref, m_ref, acc_ref,
              s0_ref, s1_ref, mc0_ref, mc1_ref, *, n_slc, n_top):
    qt = pl.program_id(1)
    t0 = qt * TQ
    rpg = N_HEADS_B // N_KV_B
    rows = rpg * TQ
    ncmp = kc_ref.shape[1]
    nwb = WINDOW_B // TQ
    s_prev = pl.multiple_of(jnp.maximum(t0 - TQ, 0), TQ)
    s_diag = pl.multiple_of(t0, TQ)
    lane = lax.broadcasted_iota(jnp.int32, (TQ, LANES), 1)
    rowi = lax.broadcasted_iota(jnp.int32, (TQ, LANES), 0)
    causal = lane <= rowi
    ones_col = (lane == 0).astype(BF16)
    has_prev = qt > 0

    c_idx = lax.broadcasted_iota(jnp.int32, (ncmp, LANES), 0)
    l_idx = lax.broadcasted_iota(jnp.int32, (ncmp, LANES), 1)
    place = ((c_idx - (TQ // CMP_STRIDE) * qt + CMP_NEAR_LO) == (l_idx & (CMP_NEAR - 1))) \
        & (l_idx < 2 * CMP_NEAR)
    kc_aug = jnp.concatenate([kc_ref[0], place.astype(BF16)], axis=1)
    qi = lax.broadcasted_iota(jnp.int32, (TQ, ncmp), 0)
    ci = lax.broadcasted_iota(jnp.int32, (TQ, ncmp), 1)
    mask_c = (t0 + qi - CMP_STRIDE * ci - (CMP_LEN - 1)) >= 0
    vc = vc_ref[0]

    cur = (t0 + rowi) // SEL_LEN
    valid = (lane <= cur) & (lane < n_slc)
    forced = (lane == 0) | (lane == cur) | (lane == cur - 1)
    n_io = lax.broadcasted_iota(jnp.int32, (n_slc, TQ), 0)
    cur0 = t0 // SEL_LEN
    near_lo = cur0 - TQ // SEL_LEN

    ks2 = jnp.concatenate(
        [jnp.concatenate([ks_ref[0, pl.ds(s_prev, TQ), :], oh_ref[pl.ds(s_prev, TQ), :]], axis=1),
         jnp.concatenate([ks_ref[0, pl.ds(s_diag, TQ), :], oh_ref[pl.ds(s_diag, TQ), :]], axis=1)],
        axis=0)
    vs2 = jnp.concatenate(
        [jnp.concatenate([vs_ref[0, pl.ds(s_prev, TQ), :], ones_col], axis=1),
         jnp.concatenate([vs_ref[0, pl.ds(s_diag, TQ), :], ones_col], axis=1)], axis=0)
    ones_far = jnp.concatenate([ones_col] * (FAR_TK // TQ), axis=0)

    w_starts = [pl.multiple_of(jnp.maximum(t0 - kb * TQ, 0), TQ) for kb in range(nwb, -1, -1)]
    kw5 = jnp.concatenate([kw_ref[0, pl.ds(st, TQ), :] for st in w_starts], axis=0)
    vw5 = jnp.concatenate([vw_ref[0, pl.ds(st, TQ), :] for st in w_starts], axis=0)

    q4s, o_cmps, q_fars = [], [], []
    for g in range(N_KV_B):
        q4 = _stack_heads(q_ref, g, rpg)
        q4s.append(q4)

        lc_all = _nt_dot(jnp.concatenate([q4, g2_ref[g]], axis=1), kc_aug)
        pcs = None
        pc_rows = []
        for r in range(rpg):
            lc = jnp.where(mask_c, lc_all[r * TQ:(r + 1) * TQ], NEG)
            e = jnp.where(mask_c, jnp.exp(lc - jnp.max(lc, axis=-1, keepdims=True)), 0.0)
            den = jnp.maximum(jnp.sum(e, axis=-1, keepdims=True), jnp.finfo(F32).tiny)
            pc = e / den
            pcs = pc if pcs is None else pcs + pc
            pc_rows.append(pc.astype(BF16))
        o_cmps.append(_dot(jnp.concatenate(pc_rows, axis=0), vc))

        pcs_hi = pcs.astype(BF16)
        pcs_lo = (pcs - pcs_hi.astype(F32)).astype(BF16)
        imp = _dot(pcs_hi, ovl_ref[...]) + _dot(pcs_lo, ovl_ref[...])
        score = jnp.where(valid, imp + jnp.where(forced, FORCE_BONUS, 0.0), NEG)
        s_t = score.T[:n_slc]
        rank = jnp.zeros((n_slc, TQ), F32)
        for mm in range(n_slc):
            sm = s_t[mm:mm + 1, :]
            beats = (sm > s_t) | ((sm == s_t) & (n_io > mm))
            rank = rank + jnp.where(beats, 1.0, 0.0)
        sel_t = jnp.where((rank < n_top) & (s_t > 0.5 * NEG), 1.0, 0.0)
        if n_slc < LANES:
            sel_t = jnp.concatenate([sel_t, jnp.zeros((LANES - n_slc, TQ), F32)], axis=0)
        sel = sel_t.T > 0.5
        m_near = jnp.where(sel & (lane >= near_lo), 0.0, NEG).astype(BF16)
        m_far = jnp.where(sel & (lane < near_lo), 0.0, NEG).astype(BF16)
        q_near = jnp.concatenate([q4, jnp.concatenate([m_near] * rpg, axis=0)], axis=1)
        q_fars.append(jnp.concatenate([q4, jnp.concatenate([m_far] * rpg, axis=0)], axis=1))

        s_all = _nt_dot(q_near, ks2)
        s_rows = []
        for r in range(rpg):
            hh = g * rpg + r
            sr = s_all[r * TQ:(r + 1) * TQ]
            s_p = jnp.where(has_prev, sr[:, :TQ] + bprev_ref[hh], NEG)
            s_d = jnp.where(causal, sr[:, TQ:] + bdiag_ref[hh], NEG)
            s_rows.append(jnp.concatenate([s_p, s_d], axis=1))
        s_near = jnp.concatenate(s_rows, axis=0)
        m0 = jnp.max(s_near, axis=-1, keepdims=True)
        m_ref[g * rows:(g + 1) * rows, :] = jnp.broadcast_to(m0, (rows, LANES))
        acc_ref[g * rows:(g + 1) * rows, :] = _dot(jnp.exp(s_near - m0).astype(BF16), vs2)

    q_far = jnp.concatenate(q_fars, axis=0)
    n_far = (qt + FAR_TK // TQ - 2) // (FAR_TK // TQ)
    last_chunk = ks_ref.shape[1] // FAR_TK - 1
    s_bufs = (s0_ref, s1_ref)
    mc_bufs = (mc0_ref, mc1_ref)

    def chunk_start(c):
        return pl.multiple_of(jnp.minimum(c, last_chunk) * FAR_TK, FAR_TK)

    def produce(slot, c):
        k0 = chunk_start(c)
        kk = jnp.concatenate([ks_ref[0, pl.ds(k0, FAR_TK), :], oh_ref[pl.ds(k0, FAR_TK), :]], axis=1)
        s = _nt_dot(q_far, kk)
        s_bufs[slot][...] = s
        mc_bufs[slot][...] = jnp.broadcast_to(jnp.max(s, axis=-1, keepdims=True), (s.shape[0], LANES))

    def consume(slot, c):
        k0 = chunk_start(c)
        vv = jnp.concatenate([vs_ref[0, pl.ds(k0, FAR_TK), :], ones_far], axis=1)
        m_old = m_ref[...]
        m_new = jnp.maximum(m_old, mc_bufs[slot][...])
        alpha = jnp.exp(m_old - m_new)
        p = jnp.exp(s_bufs[slot][...] - jnp.concatenate([m_new] * (FAR_TK // LANES), axis=1)).astype(BF16)
        acc_ref[...] = acc_ref[...] * jnp.concatenate([alpha] * 2, axis=1) + _dot(p, vv)
        m_ref[...] = m_new

    produce(0, 0)

    def far_pair(j, carry):
        produce(1, 2 * j + 1)
        consume(0, 2 * j)
        produce(0, 2 * j + 2)
        consume(1, 2 * j + 1)
        return carry

    lax.fori_loop(0, (n_far + 1) // 2, far_pair, 0)
    acc = acc_ref[...]
    o_sel_all = acc[:, :LANES] / acc[:, LANES:LANES + 1]

    for g in range(N_KV_B):
        q4 = q4s[g]
        o_cmp = o_cmps[g]
        o_sel = o_sel_all[g * rows:(g + 1) * rows]

        w_all = _nt_dot(q4, kw5)
        pw, lw = [], []
        for r in range(rpg):
            hh = g * rpg + r
            wr = w_all[r * TQ:(r + 1) * TQ]
            blocks = []
            for bi, kb in enumerate(range(nwb, -1, -1)):
                blk = wr[:, bi * TQ:(bi + 1) * TQ]
                if kb == nwb:
                    blk = jnp.where((lane > rowi) & (qt >= kb), blk, NEG)
                elif kb >= 2:
                    blk = jnp.where(qt >= kb, blk, NEG)
                elif kb == 1:
                    blk = jnp.where(has_prev, blk + bprev_ref[hh], NEG)
                else:
                    blk = jnp.where(causal, blk + bdiag_ref[hh], NEG)
                blocks.append(blk)
            sw = jnp.concatenate(blocks, axis=1)
            p = jnp.exp(sw - jnp.max(sw, axis=-1, keepdims=True))
            lw.append(jnp.sum(p, axis=-1, keepdims=True))
            pw.append(p.astype(BF16))
        o_win = _dot(jnp.concatenate(pw, axis=0), vw5)

        o_heads = []
        for r in range(rpg):
            hh = g * rpg + r
            sl = slice(r * TQ, (r + 1) * TQ)
            g_cmp = gate_ref[0, :, hh:hh + 1]
            g_sel = gate_ref[0, :, N_HEADS_B + hh:N_HEADS_B + hh + 1]
            g_win = gate_ref[0, :, 2 * N_HEADS_B + hh:2 * N_HEADS_B + hh + 1]
            o_heads.append(g_cmp * o_cmp[sl] + g_sel * o_sel[sl] + g_win * (o_win[sl] / lw[r]))
        for c, tile in enumerate(_merge_heads(o_heads, g, lane)):
            col = (g * rpg // 2 + c) * LANES
            o_ref[0, :, col:col + LANES] = tile.astype(BF16)


def _nsa(qb, kc, vc, ks, vs, kw, vw, gates, bprev, bdiag, g2):
    b, s, _ = qb.shape
    assert s % (2 * FAR_TK) == 0 and WINDOW_B % TQ == 0 and TQ % SEL_LEN == 0
    n_slc = s // SEL_LEN
    assert n_slc <= LANES and n_slc % 8 == 0
    n_top = min(SEL_TOPK, n_slc)
    ncmp = s // CMP_STRIDE
    rpg = N_HEADS_B // N_KV_B
    kpos = np.arange(s)
    onehot = (kpos[:, None] // SEL_LEN == np.arange(LANES)[None, :]).astype(np.float32)
    cstart = np.arange(ncmp) * CMP_STRIDE
    sstart = np.arange(LANES) * SEL_LEN
    ovl = ((cstart[:, None] < sstart[None, :] + SEL_LEN)
           & (cstart[:, None] + CMP_LEN - 1 >= sstart[None, :])
           & (np.arange(ncmp)[:, None] < ncmp - CMP_LEN // CMP_STRIDE + 1)
           & (np.arange(LANES)[None, :] < n_slc)).astype(np.float32)
    full = lambda i, j: (i, 0, 0)
    return pl.pallas_call(
        functools.partial(_nsa_body, n_slc=n_slc, n_top=n_top),
        out_shape=jax.ShapeDtypeStruct((b, s, QB), BF16),
        grid=(b, s // TQ),
        in_specs=[pl.BlockSpec((1, TQ, N_HEADS_B * LANES), lambda i, j: (i, j, 0)),
                  pl.BlockSpec((1, ncmp, LANES), full), pl.BlockSpec((1, ncmp, LANES), full),
                  pl.BlockSpec((1, s, LANES), full), pl.BlockSpec((1, s, LANES), full),
                  pl.BlockSpec((1, s, LANES), full), pl.BlockSpec((1, s, LANES), full),
                  pl.BlockSpec((s, LANES), lambda i, j: (0, 0)),
                  pl.BlockSpec((1, TQ, LANES), lambda i, j: (i, j, 0)),
                  pl.BlockSpec((N_HEADS_B, TQ, TQ), lambda i, j: (0, 0, 0)),
                  pl.BlockSpec((N_HEADS_B, TQ, TQ), lambda i, j: (0, 0, 0)),
                  pl.BlockSpec((N_KV_B, rpg * TQ, LANES), lambda i, j: (0, 0, 0)),
                  pl.BlockSpec((ncmp, LANES), lambda i, j: (0, 0))],
        out_specs=pl.BlockSpec((1, TQ, QB), lambda i, j: (i, j, 0)),
        scratch_shapes=[pltpu.VMEM((N_KV_B * rpg * TQ, LANES), F32),
                        pltpu.VMEM((N_KV_B * rpg * TQ, 2 * LANES), F32),
                        pltpu.VMEM((N_KV_B * rpg * TQ, FAR_TK), F32),
                        pltpu.VMEM((N_KV_B * rpg * TQ, FAR_TK), F32),
                        pltpu.VMEM((N_KV_B * rpg * TQ, LANES), F32),
                        pltpu.VMEM((N_KV_B * rpg * TQ, LANES), F32)],
        compiler_params=pltpu.CompilerParams(
            dimension_semantics=("arbitrary", "arbitrary"), vmem_limit_bytes=VMEM_LIMIT),
        name="nsa",
    )(qb, kc, vc, ks, vs, kw, vw, jnp.asarray(onehot, BF16), gates, bprev, bdiag, g2,
      jnp.asarray(ovl, BF16))


def _cmp_bias_operand(tbl_b):
    i = np.arange(TQ)[:, None]
    u = np.arange(CMP_NEAR)[None, :]
    rel = i - CMP_STRIDE * (u - CMP_NEAR_LO) - (CMP_LEN - 1)
    assert rel[0, 0] + CMP_STRIDE >= MAX_DISTANCE and rel[-1, -1] - CMP_STRIDE < 0
    near = (rel >= 0) & (rel < MAX_DISTANCE)
    far_const = tbl_b[NUM_BUCKETS - 1]
    vals = jnp.where(jnp.asarray(near)[None], _table_lookup(tbl_b, _bucket_np(rel))
                     - far_const[:, None, None], 0.0)
    hi = vals.astype(BF16)
    lo = (vals - hi.astype(F32)).astype(BF16)
    pad = jnp.zeros((N_HEADS_B, TQ, LANES - 2 * CMP_NEAR), BF16)
    out = jnp.concatenate([hi, lo, pad], axis=-1)
    rpg = N_HEADS_B // N_KV_B
    return out.reshape(N_KV_B, rpg * TQ, LANES)


def _layer(x, tbl, ffn1_norm, ffn1_w_in, ffn1_w_out, mix_norm, w_mix_in, w_mix_out,
           q_norm_a, k_norm_a, sinks_a, q_norm_b, k_norm_b,
           cmp_pos, cmp_w1, cmp_b1, cmp_w2, cmp_b2, ffn2_norm, ffn2_w_in, ffn2_w_out):
    b, s, d = x.shape
    x2d = x.reshape(b * s, d)
    x1 = _ffn(x2d, ffn1_norm, ffn1_w_in.astype(BF16), ffn1_w_out.astype(BF16))
    qa, qb, ka, ks, kw, va, vs, vw, kvc, gates = _proj(
        x1, mix_norm, w_mix_in, q_norm_a, k_norm_a, q_norm_b, k_norm_b)
    r3 = lambda a: a.reshape(b, s, a.shape[-1])
    kc, vc = _compress(r3(kvc), cmp_pos, cmp_w1, cmp_b1, cmp_w2, cmp_b2, k_norm_b)

    prev_t, diag_t = _band_tiles(tbl)
    bias_a = jnp.concatenate([prev_t[:N_HEADS_A], diag_t[:N_HEADS_A]], axis=-1)
    far_b = tbl[NUM_BUCKETS - 1, N_HEADS_A:].astype(F32)[:, None, None]
    bprev_b = prev_t[N_HEADS_A:] - far_b
    bdiag_b = diag_t[N_HEADS_A:] - far_b
    g2 = _cmp_bias_operand(tbl[:, N_HEADS_A:])

    mix_a = _swa(r3(qa), r3(ka), r3(va), bias_a, sinks_a)
    mix_b = _nsa(r3(qb), kc, vc, r3(ks), r3(vs), r3(kw), r3(vw), r3(gates), bprev_b, bdiag_b, g2)
    y = _ffn(x1, ffn2_norm, ffn2_w_in.astype(BF16), ffn2_w_out.astype(BF16),
             pre=(mix_a.reshape(b * s, QA), mix_b.reshape(b * s, QB), w_mix_out.astype(BF16)))
    return y.reshape(b, s, d)


def kernel(x, rel_bias_table, ffn1_norm, ffn1_w_in, ffn1_w_out, mix_norm, w_mix_in, w_mix_out,
           q_norm_a, k_norm_a, sinks_a, q_norm_b, k_norm_b,
           cmp_pos, cmp_w1, cmp_b1, cmp_w2, cmp_b2, ffn2_norm, ffn2_w_in, ffn2_w_out):
    for l in range(ffn1_norm.shape[0]):
        x = _layer(x, rel_bias_table, ffn1_norm[l], ffn1_w_in[l], ffn1_w_out[l],
                   mix_norm[l], w_mix_in[l], w_mix_out[l],
                   q_norm_a[l], k_norm_a[l], sinks_a[l], q_norm_b[l], k_norm_b[l],
                   cmp_pos[l], cmp_w1[l], cmp_b1[l], cmp_w2[l], cmp_b2[l],
                   ffn2_norm[l], ffn2_w_in[l], ffn2_w_out[l])
    return x
```

```python
import functools
import math

import numpy as np
import jax
import jax.numpy as jnp
from jax import lax
from jax.experimental import pallas as pl
from jax.experimental.pallas import tpu as pltpu

F32 = jnp.float32
BF16 = jnp.bfloat16

D_MODEL = 1024
HEAD_DIM = 64
N_HEADS_A = 8
N_KV_A = 2
N_HEADS_B = 8
N_KV_B = 2
N_HEADS = N_HEADS_A + N_HEADS_B
WINDOW_A = 128
CMP_LEN = 32
CMP_STRIDE = 16
CMP_HIDDEN = 128
SEL_LEN = 64
SEL_TOPK = 16
WINDOW_B = 512
FORCE_BONUS = 1000.0
NUM_BUCKETS = 32
MAX_DISTANCE = 128
D_FF = 2816
EPS = 1e-6
NEG = -1e9

QA = N_HEADS_A * HEAD_DIM
KVA = N_KV_A * HEAD_DIM
QB = N_HEADS_B * HEAD_DIM
KVB = N_KV_B * HEAD_DIM
N_GATES = 3 * N_HEADS_B
OFF_KA = QA
OFF_VA = OFF_KA + KVA
OFF_QB = OFF_VA + KVA
OFF_KVB = OFF_QB + QB
OFF_GB = OFF_KVB + 6 * KVB
D_IN = OFF_GB + N_GATES

LANES = 128
TQ = 128
FAR_TK = 512
FFN_TM = 512
FFN_TF = 256
VMEM_LIMIT = 56 * 1024 * 1024

P_QA = 0
P_QB = P_QA + N_HEADS_A * LANES
P_KA = P_QB + N_HEADS_B * LANES
P_KS = P_KA + LANES
P_KW = P_KS + LANES
P_VA = P_KW + LANES
P_VS = P_VA + LANES
P_VW = P_VS + LANES
P_KVC = P_VW + LANES
P_GATE = P_KVC + 2 * LANES
P_TOTAL = P_GATE + LANES
P_NORM = P_VA


def _nt_dot(a, b):
    return lax.dot_general(a, b, (((1,), (1,)), ((), ())), preferred_element_type=F32)


def _dot(a, b):
    return jnp.dot(a, b, preferred_element_type=F32)


def _rms_rows(x, g):
    ms = jnp.mean(x * x, axis=-1, keepdims=True)
    return x * lax.rsqrt(ms + EPS) * g


def _bucket_np(rel):
    n = np.maximum(rel, 0)
    max_exact = NUM_BUCKETS // 2
    nf = np.maximum(n, 1).astype(np.float64)
    large = max_exact + (np.log(nf / max_exact) / math.log(MAX_DISTANCE / max_exact)
                         * (NUM_BUCKETS - max_exact)).astype(np.int64)
    large = np.minimum(large, NUM_BUCKETS - 1)
    return np.where(n < max_exact, n, large).astype(np.int32)


def _ffn_body(*refs, pre):
    if pre:
        x_ref, ma_ref, mb_ref, wo_ref, nrm_ref, win_ref, wout_ref, o_ref, acc_ref = refs
    else:
        x_ref, nrm_ref, win_ref, wout_ref, o_ref, acc_ref = refs
    x = x_ref[...]
    if pre:
        x = x + _dot(ma_ref[...], wo_ref[0:QA, :]) + _dot(mb_ref[...], wo_ref[QA:QA + QB, :])
    h = _rms_rows(x, nrm_ref[...]).astype(BF16)
    for c in range(D_FF // FFN_TF):
        gate = _dot(h, win_ref[:, c * FFN_TF:(c + 1) * FFN_TF])
        up = _dot(h, win_ref[:, D_FF + c * FFN_TF:D_FF + (c + 1) * FFN_TF])
        a = (gate * jax.nn.sigmoid(gate) * up).astype(BF16)
        part = _dot(a, wout_ref[c * FFN_TF:(c + 1) * FFN_TF, :])
        if c == 0:
            acc_ref[...] = part
        else:
            acc_ref[...] += part
    o_ref[...] = x + 0.5 * acc_ref[...]


def _ffn(x2d, nrm, w_in, w_out, pre=None):
    m = x2d.shape[0]
    tm = FFN_TM
    assert m % tm == 0
    row = lambda i: (i, 0)
    const = lambda i: (0, 0)
    in_specs = [pl.BlockSpec((tm, D_MODEL), row)]
    args = [x2d]
    if pre is not None:
        mix_a, mix_b, wo = pre
        in_specs += [pl.BlockSpec((tm, QA), row), pl.BlockSpec((tm, QB), row),
                     pl.BlockSpec((QA + QB, D_MODEL), const)]
        args += [mix_a, mix_b, wo]
    in_specs += [pl.BlockSpec((1, D_MODEL), const),
                 pl.BlockSpec((D_MODEL, 2 * D_FF), const),
                 pl.BlockSpec((D_FF, D_MODEL), const)]
    args += [nrm.reshape(1, D_MODEL), w_in, w_out]
    return pl.pallas_call(
        functools.partial(_ffn_body, pre=pre is not None),
        out_shape=jax.ShapeDtypeStruct((m, D_MODEL), F32),
        grid=(m // tm,),
        in_specs=in_specs,
        out_specs=pl.BlockSpec((tm, D_MODEL), row),
        scratch_shapes=[pltpu.VMEM((tm, D_MODEL), F32)],
        compiler_params=pltpu.CompilerParams(
            dimension_semantics=("arbitrary",), vmem_limit_bytes=VMEM_LIMIT),
        name="ffn_pre" if pre is not None else "ffn",
    )(*args)


def _proj_body(x_ref, nrm_ref, w_ref, gain_ref, qa_ref, qb_ref, ka_ref, ks_ref, kw_ref,
               va_ref, vs_ref, vw_ref, kvc_ref, gate_ref):
    tm = x_ref.shape[0]
    h = _rms_rows(x_ref[...], nrm_ref[...]).astype(BF16)
    lo = lax.broadcasted_iota(jnp.int32, (tm, LANES), 1) < HEAD_DIM
    inv_d = 1.0 / HEAD_DIM

    def pair(c0):
        p2 = _dot(h, w_ref[:, c0:c0 + 2 * LANES])
        return p2[:, :LANES], p2[:, LANES:]

    def one_head(p, c0):
        ms = jnp.sum(p * p, axis=-1, keepdims=True) * inv_d
        return (p * lax.rsqrt(ms + EPS) * gain_ref[:, c0:c0 + LANES]).astype(BF16)

    def two_heads(p, c0):
        sq = p * p
        ms0 = jnp.sum(jnp.where(lo, sq, 0.0), axis=-1, keepdims=True) * inv_d
        ms1 = jnp.sum(jnp.where(lo, 0.0, sq), axis=-1, keepdims=True) * inv_d
        r = jnp.where(lo, lax.rsqrt(ms0 + EPS), lax.rsqrt(ms1 + EPS))
        return (p * r * gain_ref[:, c0:c0 + LANES]).astype(BF16)

    for hp in range(N_HEADS // 2):
        c0 = 2 * hp * LANES
        pa, pb = pair(c0)
        ref = qa_ref if c0 < P_QB else qb_ref
        l0 = c0 - (P_QA if c0 < P_QB else P_QB)
        ref[:, l0:l0 + LANES] = one_head(pa, c0)
        ref[:, l0 + LANES:l0 + 2 * LANES] = one_head(pb, c0 + LANES)
    pa, pb = pair(P_KA)
    ka_ref[...] = two_heads(pa, P_KA)
    ks_ref[...] = two_heads(pb, P_KS)
    pa, pb = pair(P_KW)
    kw_ref[...] = two_heads(pa, P_KW)
    va_ref[...] = pb.astype(BF16)
    pa, pb = pair(P_VS)
    vs_ref[...] = pa.astype(BF16)
    vw_ref[...] = pb.astype(BF16)
    pa, pb = pair(P_KVC)
    kvc_ref[:, :LANES] = pa
    kvc_ref[:, LANES:] = pb
    pg = _dot(h, w_ref[:, P_GATE:P_GATE + LANES])
    gate_ref[...] = jax.nn.sigmoid(pg)


def _proj_src_columns():
    src = np.full((P_TOTAL,), -1, np.int64)
    gidx = np.full((P_NORM,), -1, np.int64)
    d = np.arange(HEAD_DIM)
    for hh in range(N_HEADS_A):
        g = hh // (N_HEADS_A // N_KV_A)
        src[P_QA + hh * LANES + g * HEAD_DIM + d] = hh * HEAD_DIM + d
        gidx[P_QA + hh * LANES + g * HEAD_DIM + d] = d
    for hh in range(N_HEADS_B):
        g = hh // (N_HEADS_B // N_KV_B)
        src[P_QB + hh * LANES + g * HEAD_DIM + d] = OFF_QB + hh * HEAD_DIM + d
        gidx[P_QB + hh * LANES + g * HEAD_DIM + d] = HEAD_DIM + d
    two = np.arange(2 * HEAD_DIM)
    src[P_KA + two] = OFF_KA + two
    gidx[P_KA + two] = 2 * HEAD_DIM + two % HEAD_DIM
    src[P_VA + two] = OFF_VA + two
    for slot, dst in ((0, P_KVC), (1, P_KVC + LANES), (2, P_KS), (3, P_VS), (4, P_KW), (5, P_VW)):
        src[dst + two] = OFF_KVB + slot * KVB + two
    gidx[P_KS + two] = 3 * HEAD_DIM + two % HEAD_DIM
    gidx[P_KW + two] = 3 * HEAD_DIM + two % HEAD_DIM
    src[P_GATE + np.arange(N_GATES)] = OFF_GB + np.arange(N_GATES)
    return src, gidx


def _proj(x2d, nrm, w_mix_in, q_norm_a, k_norm_a, q_norm_b, k_norm_b):
    m = x2d.shape[0]
    tm = FFN_TM
    src, gidx = _proj_src_columns()
    pieces, c = [], 0
    while c < P_TOTAL:
        e = c + 1
        while e < P_TOTAL and ((src[e] < 0) == (src[c] < 0)) and (src[c] < 0 or src[e] == src[e - 1] + 1):
            e += 1
        pieces.append(jnp.zeros((D_MODEL, e - c), w_mix_in.dtype) if src[c] < 0
                      else w_mix_in[:, src[c]:src[c] + e - c])
        c = e
    w = jnp.concatenate(pieces, axis=1).astype(BF16)
    scale = HEAD_DIM ** -0.5
    gsrc = jnp.concatenate([q_norm_a * scale, q_norm_b * scale, k_norm_a, k_norm_b]).astype(F32)
    gain = jnp.where(jnp.asarray(gidx >= 0), gsrc[np.maximum(gidx, 0)], 0.0).reshape(1, P_NORM)
    row = lambda i: (i, 0)
    const = lambda i: (0, 0)
    widths = [(N_HEADS_A * LANES, BF16), (N_HEADS_B * LANES, BF16)] + [(LANES, BF16)] * 6 + \
             [(2 * LANES, F32), (LANES, F32)]
    return pl.pallas_call(
        _proj_body,
        out_shape=[jax.ShapeDtypeStruct((m, wd), dt) for wd, dt in widths],
        grid=(m // tm,),
        in_specs=[pl.BlockSpec((tm, D_MODEL), row), pl.BlockSpec((1, D_MODEL), const),
                  pl.BlockSpec((D_MODEL, P_TOTAL), const), pl.BlockSpec((1, P_NORM), const)],
        out_specs=[pl.BlockSpec((tm, wd), row) for wd, _ in widths],
        compiler_params=pltpu.CompilerParams(
            dimension_semantics=("arbitrary",), vmem_limit_bytes=VMEM_LIMIT),
        name="proj",
    )(x2d, nrm.reshape(1, D_MODEL), w, gain)


CMP_FEAT = CMP_STRIDE * 2 * LANES
CMP_SLOTS = 4


def _compress_body(x_ref, pos_ref, w1_ref, b1_ref, w2_ref, b2_ref, gain_ref, kc_ref, vc_ref):
    nck = x_ref.shape[1]
    x = x_ref[0]
    a = _dot((x + pos_ref[0]).astype(BF16), w1_ref[0])
    b = _dot((x + pos_ref[1]).astype(BF16), w1_ref[1])
    pre = a + pltpu.roll(b, nck - 1, 0) + b1_ref[...]
    hid = (pre * jax.nn.sigmoid(pre)).astype(BF16)
    out = _dot(hid, w2_ref[...]) + b2_ref[...]
    k = out[:, :LANES]
    lo = lax.broadcasted_iota(jnp.int32, (nck, LANES), 1) < HEAD_DIM
    sq = k * k
    ms0 = jnp.sum(jnp.where(lo, sq, 0.0), axis=-1, keepdims=True) * (1.0 / HEAD_DIM)
    ms1 = jnp.sum(jnp.where(lo, 0.0, sq), axis=-1, keepdims=True) * (1.0 / HEAD_DIM)
    r = jnp.where(lo, lax.rsqrt(ms0 + EPS), lax.rsqrt(ms1 + EPS))
    kc_ref[0] = (k * r * gain_ref[...]).astype(BF16)
    vc_ref[0] = out[:, LANES:].astype(BF16)


def _compress(kvc, cmp_pos, cmp_w1, cmp_b1, cmp_w2, cmp_b2, k_norm_b):
    b, s, _ = kvc.shape
    nck = s // CMP_STRIDE
    x = kvc.reshape(b, nck, CMP_FEAT)
    r = CMP_LEN // CMP_STRIDE
    w1r = cmp_w1.reshape(2, r, CMP_STRIDE, HEAD_DIM, CMP_HIDDEN)
    eye = jnp.eye(CMP_SLOTS, dtype=F32)
    kv_of_slot = np.array([0, 0, 1, 1])
    w1s = w1r[kv_of_slot]
    w1big = jnp.einsum('sjpdh,st->jpsdth', w1s, eye).reshape(r, CMP_FEAT, CMP_SLOTS * CMP_HIDDEN)
    posr = cmp_pos.reshape(2, r, CMP_STRIDE, HEAD_DIM)[kv_of_slot]
    posbig = posr.transpose(1, 2, 0, 3).reshape(r, 1, CMP_FEAT)
    b1big = cmp_b1[kv_of_slot].reshape(1, CMP_SLOTS * CMP_HIDDEN)
    w2big = jnp.einsum('shd,st->shtd', cmp_w2[kv_of_slot], eye).reshape(
        CMP_SLOTS * CMP_HIDDEN, CMP_SLOTS * HEAD_DIM)
    b2big = cmp_b2[kv_of_slot].reshape(1, CMP_SLOTS * HEAD_DIM)
    gain = jnp.tile(k_norm_b, 2).reshape(1, LANES)
    assert r == 2
    c3 = lambda i: (0, 0, 0)
    c2 = lambda i: (0, 0)
    return pl.pallas_call(
        _compress_body,
        out_shape=[jax.ShapeDtypeStruct((b, nck, LANES), BF16)] * 2,
        grid=(b,),
        in_specs=[pl.BlockSpec((1, nck, CMP_FEAT), lambda i: (i, 0, 0)),
                  pl.BlockSpec((r, 1, CMP_FEAT), c3),
                  pl.BlockSpec((r, CMP_FEAT, CMP_SLOTS * CMP_HIDDEN), c3),
                  pl.BlockSpec((1, CMP_SLOTS * CMP_HIDDEN), c2),
                  pl.BlockSpec((CMP_SLOTS * CMP_HIDDEN, CMP_SLOTS * HEAD_DIM), c2),
                  pl.BlockSpec((1, CMP_SLOTS * HEAD_DIM), c2),
                  pl.BlockSpec((1, LANES), c2)],
        out_specs=[pl.BlockSpec((1, nck, LANES), lambda i: (i, 0, 0))] * 2,
        compiler_params=pltpu.CompilerParams(
            dimension_semantics=("arbitrary",), vmem_limit_bytes=VMEM_LIMIT),
        name="compress",
    )(x, posbig, w1big.astype(BF16), b1big, w2big.astype(BF16), b2big, gain)


def _stack_heads(q_ref, g, heads_per_group):
    return jnp.concatenate(
        [q_ref[0, :, (g * heads_per_group + r) * LANES:(g * heads_per_group + r + 1) * LANES]
         for r in range(heads_per_group)], axis=0)


def _merge_heads(o_heads, g, lane):
    tiles = []
    for c in range(len(o_heads) // 2):
        lo_src = o_heads[2 * c] if g == 0 else pltpu.roll(o_heads[2 * c], HEAD_DIM, 1)
        hi_src = o_heads[2 * c + 1] if g == 1 else pltpu.roll(o_heads[2 * c + 1], HEAD_DIM, 1)
        tiles.append(jnp.where(lane < HEAD_DIM, lo_src, hi_src))
    return tiles


def _band_tiles(tbl):
    i = np.arange(TQ)[:, None]
    j = np.arange(TQ)[None, :]
    return _table_lookup(tbl, _bucket_np(i - j + TQ)), _table_lookup(tbl, _bucket_np(i - j))


def _table_lookup(tbl, buckets):
    onehot = (buckets.reshape(-1)[:, None] == np.arange(NUM_BUCKETS)[None, :]).astype(np.float32)
    out = jnp.einsum('pk,kh->hp', jnp.asarray(onehot), tbl.astype(F32), precision=lax.Precision.HIGHEST)
    return out.reshape((tbl.shape[1],) + buckets.shape)


def _swa_body(sink_ref, q_ref, k_ref, v_ref, bias_ref, o_ref):
    qt = pl.program_id(1)
    t0 = qt * TQ
    s_prev = pl.multiple_of(jnp.maximum(t0 - TQ, 0), TQ)
    s_diag = pl.multiple_of(t0, TQ)
    k2 = jnp.concatenate([k_ref[0, pl.ds(s_prev, TQ), :], k_ref[0, pl.ds(s_diag, TQ), :]], axis=0)
    v2 = jnp.concatenate([v_ref[0, pl.ds(s_prev, TQ), :], v_ref[0, pl.ds(s_diag, TQ), :]], axis=0)
    ii = lax.broadcasted_iota(jnp.int32, (TQ, 2 * TQ), 0)
    jj = lax.broadcasted_iota(jnp.int32, (TQ, 2 * TQ), 1)
    mask = ((jj < TQ) & (jj > ii) & (qt > 0)) | ((jj >= TQ) & ((jj - TQ) <= ii))
    lane = lax.broadcasted_iota(jnp.int32, (TQ, LANES), 1)
    rpg = N_HEADS_A // N_KV_A
    for g in range(N_KV_A):
        q4 = _stack_heads(q_ref, g, rpg)
        s_all = _nt_dot(q4, k2)
        ps, dens = [], []
        for r in range(rpg):
            hh = g * rpg + r
            s = jnp.where(mask, s_all[r * TQ:(r + 1) * TQ] + bias_ref[hh], NEG)
            sink = sink_ref[hh]
            m = jnp.maximum(jnp.max(s, axis=-1, keepdims=True), sink)
            p = jnp.exp(s - m)
            dens.append(jnp.sum(p, axis=-1, keepdims=True) + jnp.exp(sink - m))
            ps.append(p.astype(BF16))
        o_all = _dot(jnp.concatenate(ps, axis=0), v2)
        o_heads = [o_all[r * TQ:(r + 1) * TQ] / dens[r] for r in range(rpg)]
        for c, tile in enumerate(_merge_heads(o_heads, g, lane)):
            col = (g * rpg // 2 + c) * LANES
            o_ref[0, :, col:col + LANES] = tile.astype(BF16)


def _swa(qa, ka, va, bias, sinks):
    b, s, _ = qa.shape
    assert WINDOW_A == TQ
    return pl.pallas_call(
        _swa_body,
        out_shape=jax.ShapeDtypeStruct((b, s, QA), BF16),
        grid=(b, s // TQ),
        in_specs=[pl.BlockSpec(memory_space=pltpu.SMEM),
                  pl.BlockSpec((1, TQ, N_HEADS_A * LANES), lambda i, j: (i, j, 0)),
                  pl.BlockSpec((1, s, LANES), lambda i, j: (i, 0, 0)),
                  pl.BlockSpec((1, s, LANES), lambda i, j: (i, 0, 0)),
                  pl.BlockSpec((N_HEADS_A, TQ, 2 * TQ), lambda i, j: (0, 0, 0))],
        out_specs=pl.BlockSpec((1, TQ, QA), lambda i, j: (i, j, 0)),
        compiler_params=pltpu.CompilerParams(
            dimension_semantics=("arbitrary", "arbitrary"), vmem_limit_bytes=VMEM_LIMIT),
        name="swa",
    )(sinks.astype(F32), qa, ka, va, bias)


CMP_NEAR = 16
CMP_NEAR_LO = 9


def _nsa_body(q_ref, kc_ref, vc_ref, ks_ref, vs_ref, kw_ref, vw_ref, oh_ref, gate_ref,
              bprev_ref, bdiag_ref, g2_ref, ovl_ref, o_ref, m_ref, acc_ref,
              s0_ref, s1_ref, mc0_ref, mc1_ref, *, n_slc, n_top):
    qt = pl.program_id(1)
    t0 = qt * TQ
    rpg = N_HEADS_B // N_KV_B
    rows = rpg * TQ
    ncmp = kc_ref.shape[1]
    nwb = WINDOW_B // TQ
    s_prev = pl.multiple_of(jnp.maximum(t0 - TQ, 0), TQ)
    s_diag = pl.multiple_of(t0, TQ)
    lane = lax.broadcasted_iota(jnp.int32, (TQ, LANES), 1)
    rowi = lax.broadcasted_iota(jnp.int32, (TQ, LANES), 0)
    causal = lane <= rowi
    ones_col = (lane == 0).astype(BF16)
    has_prev = qt > 0

    c_idx = lax.broadcasted_iota(jnp.int32, (ncmp, LANES), 0)
    l_idx = lax.broadcasted_iota(jnp.int32, (ncmp, LANES), 1)
    place = ((c_idx - (TQ // CMP_STRIDE) * qt + CMP_NEAR_LO) == (l_idx & (CMP_NEAR - 1))) \
        & (l_idx < 2 * CMP_NEAR)
    kc_aug = jnp.concatenate([kc_ref[0], place.astype(BF16)], axis=1)
    qi = lax.broadcasted_iota(jnp.int32, (TQ, ncmp), 0)
    ci = lax.broadcasted_iota(jnp.int32, (TQ, ncmp), 1)
    mask_c = (t0 + qi - CMP_STRIDE * ci - (CMP_LEN - 1)) >= 0
    vc = vc_ref[0]

    cur = (t0 + rowi) // SEL_LEN
    valid = (lane <= cur) & (lane < n_slc)
    forced = (lane == 0) | (lane == cur) | (lane == cur - 1)
    cur0 = t0 // SEL_LEN
    near_lo = cur0 - TQ // SEL_LEN

    ks2 = jnp.concatenate(
        [jnp.concatenate([ks_ref[0, pl.ds(s_prev, TQ), :], oh_ref[pl.ds(s_prev, TQ), :]], axis=1),
         jnp.concatenate([ks_ref[0, pl.ds(s_diag, TQ), :], oh_ref[pl.ds(s_diag, TQ), :]], axis=1)],
        axis=0)
    vs2 = jnp.concatenate(
        [jnp.concatenate([vs_ref[0, pl.ds(s_prev, TQ), :], ones_col], axis=1),
         jnp.concatenate([vs_ref[0, pl.ds(s_diag, TQ), :], ones_col], axis=1)], axis=0)
    ones_far = jnp.concatenate([ones_col] * (FAR_TK // TQ), axis=0)

    w_starts = [pl.multiple_of(jnp.maximum(t0 - kb * TQ, 0), TQ) for kb in range(nwb, -1, -1)]
    kw5 = jnp.concatenate([kw_ref[0, pl.ds(st, TQ), :] for st in w_starts], axis=0)
    vw5 = jnp.concatenate([vw_ref[0, pl.ds(st, TQ), :] for st in w_starts], axis=0)

    nh = N_HEADS_B
    q_all = jnp.concatenate([q_ref[0, :, hh * LANES:(hh + 1) * LANES] for hh in range(nh)], axis=0)
    g2_all = jnp.concatenate([g2_ref[g] for g in range(N_KV_B)], axis=0)

    lc_all = _nt_dot(jnp.concatenate([q_all, g2_all], axis=1), kc_aug)
    pcs = [None] * N_KV_B
    pc_rows = []
    for hh in range(nh):
        lc = jnp.where(mask_c, lc_all[hh * TQ:(hh + 1) * TQ], NEG)
        e = jnp.where(mask_c, jnp.exp(lc - jnp.max(lc, axis=-1, keepdims=True)), 0.0)
        den = jnp.maximum(jnp.sum(e, axis=-1, keepdims=True), jnp.finfo(F32).tiny)
        pc = e / den
        pcs[hh // rpg] = pc if pcs[hh // rpg] is None else pcs[hh // rpg] + pc
        pc_rows.append(pc.astype(BF16))
    o_cmp_all = _dot(jnp.concatenate(pc_rows, axis=0), vc)

    pcs_all = jnp.concatenate(pcs, axis=0)
    pcs_hi = pcs_all.astype(BF16)
    pcs_lo = (pcs_all - pcs_hi.astype(F32)).astype(BF16)
    imp_all = _dot(pcs_hi, ovl_ref[...]) + _dot(pcs_lo, ovl_ref[...])
    s_ts = []
    for g in range(N_KV_B):
        score = jnp.where(valid, imp_all[g * TQ:(g + 1) * TQ] + jnp.where(forced, FORCE_BONUS, 0.0), NEG)
        s_ts.append(score.T[:n_slc])
    s_t = jnp.concatenate(s_ts, axis=1)
    n_io = lax.broadcasted_iota(jnp.int32, (n_slc, N_KV_B * TQ), 0)
    n_acc = 4
    ranks = [jnp.zeros((n_slc, N_KV_B * TQ), F32) for _ in range(n_acc)]
    for mm in range(n_slc):
        sm = s_t[mm:mm + 1, :]
        beats = (sm > s_t) | ((sm == s_t) & (n_io > mm))
        ranks[mm % n_acc] = ranks[mm % n_acc] + jnp.where(beats, 1.0, 0.0)
    rank = (ranks[0] + ranks[1]) + (ranks[2] + ranks[3])
    sel_t = jnp.where((rank < n_top) & (s_t > 0.5 * NEG), 1.0, 0.0)
    m_nears, m_fars = [], []
    for g in range(N_KV_B):
        st = sel_t[:, g * TQ:(g + 1) * TQ]
        if n_slc < LANES:
            st = jnp.concatenate([st, jnp.zeros((LANES - n_slc, TQ), F32)], axis=0)
        sel = st.T > 0.5
        m_nears.append(jnp.where(sel & (lane >= near_lo), 0.0, NEG).astype(BF16))
        m_fars.append(jnp.where(sel & (lane < near_lo), 0.0, NEG).astype(BF16))
    q_near = jnp.concatenate(
        [q_all, jnp.concatenate([m_nears[hh // rpg] for hh in range(nh)], axis=0)], axis=1)
    q_far = jnp.concatenate(
        [q_all, jnp.concatenate([m_fars[hh // rpg] for hh in range(nh)], axis=0)], axis=1)

    s_all = _nt_dot(q_near, ks2)
    s_rows = []
    for hh in range(nh):
        sr = s_all[hh * TQ:(hh + 1) * TQ]
        s_p = jnp.where(has_prev, sr[:, :TQ] + bprev_ref[hh], NEG)
        s_d = jnp.where(causal, sr[:, TQ:] + bdiag_ref[hh], NEG)
        s_rows.append(jnp.concatenate([s_p, s_d], axis=1))
    s_near = jnp.concatenate(s_rows, axis=0)
    m0 = jnp.max(s_near, axis=-1, keepdims=True)
    m_ref[...] = jnp.broadcast_to(m0, (nh * TQ, LANES))
    acc_ref[...] = _dot(jnp.exp(s_near - m0).astype(BF16), vs2)

    n_far = (qt + FAR_TK // TQ - 2) // (FAR_TK // TQ)
    last_chunk = ks_ref.shape[1] // FAR_TK - 1
    s_bufs = (s0_ref, s1_ref)
    mc_bufs = (mc0_ref, mc1_ref)

    def chunk_start(c):
        return pl.multiple_of(jnp.minimum(c, last_chunk) * FAR_TK, FAR_TK)

    def produce(slot, c):
        k0 = chunk_start(c)
        kk = jnp.concatenate([ks_ref[0, pl.ds(k0, FAR_TK), :], oh_ref[pl.ds(k0, FAR_TK), :]], axis=1)
        s = _nt_dot(q_far, kk)
        s_bufs[slot][...] = s
        mc_bufs[slot][...] = jnp.broadcast_to(jnp.max(s, axis=-1, keepdims=True), (s.shape[0], LANES))

    def consume(slot, c):
        k0 = chunk_start(c)
        vv = jnp.concatenate([vs_ref[0, pl.ds(k0, FAR_TK), :], ones_far], axis=1)
        m_old = m_ref[...]
        m_new = jnp.maximum(m_old, mc_bufs[slot][...])
        alpha = jnp.exp(m_old - m_new)
        p = jnp.exp(s_bufs[slot][...] - jnp.concatenate([m_new] * (FAR_TK // LANES), axis=1)).astype(BF16)
        acc_ref[...] = acc_ref[...] * jnp.concatenate([alpha] * 2, axis=1) + _dot(p, vv)
        m_ref[...] = m_new

    produce(0, 0)

    def far_pair(j, carry):
        produce(1, 2 * j + 1)
        consume(0, 2 * j)
        produce(0, 2 * j + 2)
        consume(1, 2 * j + 1)
        return carry

    lax.fori_loop(0, (n_far + 1) // 2, far_pair, 0)
    acc = acc_ref[...]
    o_sel_all = acc[:, :LANES] / acc[:, LANES:LANES + 1]

    w_all = _nt_dot(q_all, kw5)
    pw, lw = [], []
    for hh in range(nh):
        wr = w_all[hh * TQ:(hh + 1) * TQ]
        blocks = []
        for bi, kb in enumerate(range(nwb, -1, -1)):
            blk = wr[:, bi * TQ:(bi + 1) * TQ]
            if kb == nwb:
                blk = jnp.where((lane > rowi) & (qt >= kb), blk, NEG)
            elif kb >= 2:
                blk = jnp.where(qt >= kb, blk, NEG)
            elif kb == 1:
                blk = jnp.where(has_prev, blk + bprev_ref[hh], NEG)
            else:
                blk = jnp.where(causal, blk + bdiag_ref[hh], NEG)
            blocks.append(blk)
        sw = jnp.concatenate(blocks, axis=1)
        p = jnp.exp(sw - jnp.max(sw, axis=-1, keepdims=True))
        lw.append(jnp.sum(p, axis=-1, keepdims=True))
        pw.append(p.astype(BF16))
    o_win_all = _dot(jnp.concatenate(pw, axis=0), vw5)

    for g in range(N_KV_B):
        o_heads = []
        for r in range(rpg):
            hh = g * rpg + r
            sl = slice(hh * TQ, (hh + 1) * TQ)
            g_cmp = gate_ref[0, :, hh:hh + 1]
            g_sel = gate_ref[0, :, N_HEADS_B + hh:N_HEADS_B + hh + 1]
            g_win = gate_ref[0, :, 2 * N_HEADS_B + hh:2 * N_HEADS_B + hh + 1]
            o_heads.append(g_cmp * o_cmp_all[sl] + g_sel * o_sel_all[sl] + g_win * (o_win_all[sl] / lw[hh]))
        for c, tile in enumerate(_merge_heads(o_heads, g, lane)):
            col = (g * rpg // 2 + c) * LANES
            o_ref[0, :, col:col + LANES] = tile.astype(BF16)


def _nsa(qb, kc, vc, ks, vs, kw, vw, gates, bprev, bdiag, g2):
    b, s, _ = qb.shape
    assert s % (2 * FAR_TK) == 0 and WINDOW_B % TQ == 0 and TQ % SEL_LEN == 0
    n_slc = s // SEL_LEN
    assert n_slc <= LANES and n_slc % 8 == 0
    n_top = min(SEL_TOPK, n_slc)
    ncmp = s // CMP_STRIDE
    rpg = N_HEADS_B // N_KV_B
    kpos = np.arange(s)
    onehot = (kpos[:, None] // SEL_LEN == np.arange(LANES)[None, :]).astype(np.float32)
    cstart = np.arange(ncmp) * CMP_STRIDE
    sstart = np.arange(LANES) * SEL_LEN
    ovl = ((cstart[:, None] < sstart[None, :] + SEL_LEN)
           & (cstart[:, None] + CMP_LEN - 1 >= sstart[None, :])
           & (np.arange(ncmp)[:, None] < ncmp - CMP_LEN // CMP_STRIDE + 1)
           & (np.arange(LANES)[None, :] < n_slc)).astype(np.float32)
    full = lambda i, j: (i, 0, 0)
    return pl.pallas_call(
        functools.partial(_nsa_body, n_slc=n_slc, n_top=n_top),
        out_shape=jax.ShapeDtypeStruct((b, s, QB), BF16),
        grid=(b, s // TQ),
        in_specs=[pl.BlockSpec((1, TQ, N_HEADS_B * LANES), lambda i, j: (i, j, 0)),
                  pl.BlockSpec((1, ncmp, LANES), full), pl.BlockSpec((1, ncmp, LANES), full),
                  pl.BlockSpec((1, s, LANES), full), pl.BlockSpec((1, s, LANES), full),
                  pl.BlockSpec((1, s, LANES), full), pl.BlockSpec((1, s, LANES), full),
                  pl.BlockSpec((s, LANES), lambda i, j: (0, 0)),
                  pl.BlockSpec((1, TQ, LANES), lambda i, j: (i, j, 0)),
                  pl.BlockSpec((N_HEADS_B, TQ, TQ), lambda i, j: (0, 0, 0)),
                  pl.BlockSpec((N_HEADS_B, TQ, TQ), lambda i, j: (0, 0, 0)),
                  pl.BlockSpec((N_KV_B, rpg * TQ, LANES), lambda i, j: (0, 0, 0)),
                  pl.BlockSpec((ncmp, LANES), lambda i, j: (0, 0))],
        out_specs=pl.BlockSpec((1, TQ, QB), lambda i, j: (i, j, 0)),
        scratch_shapes=[pltpu.VMEM((N_KV_B * rpg * TQ, LANES), F32),
                        pltpu.VMEM((N_KV_B * rpg * TQ, 2 * LANES), F32),
                        pltpu.VMEM((N_KV_B * rpg * TQ, FAR_TK), F32),
                        pltpu.VMEM((N_KV_B * rpg * TQ, FAR_TK), F32),
                        pltpu.VMEM((N_KV_B * rpg * TQ, LANES), F32),
                        pltpu.VMEM((N_KV_B * rpg * TQ, LANES), F32)],
        compiler_params=pltpu.CompilerParams(
            dimension_semantics=("arbitrary", "arbitrary"), vmem_limit_bytes=VMEM_LIMIT),
        name="nsa",
    )(qb, kc, vc, ks, vs, kw, vw, jnp.asarray(onehot, BF16), gates, bprev, bdiag, g2,
      jnp.asarray(ovl, BF16))


def _cmp_bias_operand(tbl_b):
    i = np.arange(TQ)[:, None]
    u = np.arange(CMP_NEAR)[None, :]
    rel = i - CMP_STRIDE * (u - CMP_NEAR_LO) - (CMP_LEN - 1)
    assert rel[0, 0] + CMP_STRIDE >= MAX_DISTANCE and rel[-1, -1] - CMP_STRIDE < 0
    near = (rel >= 0) & (rel < MAX_DISTANCE)
    far_const = tbl_b[NUM_BUCKETS - 1]
    vals = jnp.where(jnp.asarray(near)[None], _table_lookup(tbl_b, _bucket_np(rel))
                     - far_const[:, None, None], 0.0)
    hi = vals.astype(BF16)
    lo = (vals - hi.astype(F32)).astype(BF16)
    pad = jnp.zeros((N_HEADS_B, TQ, LANES - 2 * CMP_NEAR), BF16)
    out = jnp.concatenate([hi, lo, pad], axis=-1)
    rpg = N_HEADS_B // N_KV_B
    return out.reshape(N_KV_B, rpg * TQ, LANES)


def _layer(x, tbl, ffn1_norm, ffn1_w_in, ffn1_w_out, mix_norm, w_mix_in, w_mix_out,
           q_norm_a, k_norm_a, sinks_a, q_norm_b, k_norm_b,
           cmp_pos, cmp_w1, cmp_b1, cmp_w2, cmp_b2, ffn2_norm, ffn2_w_in, ffn2_w_out):
    b, s, d = x.shape
    x2d = x.reshape(b * s, d)
    x1 = _ffn(x2d, ffn1_norm, ffn1_w_in.astype(BF16), ffn1_w_out.astype(BF16))
    qa, qb, ka, ks, kw, va, vs, vw, kvc, gates = _proj(
        x1, mix_norm, w_mix_in, q_norm_a, k_norm_a, q_norm_b, k_norm_b)
    r3 = lambda a: a.reshape(b, s, a.shape[-1])
    kc, vc = _compress(r3(kvc), cmp_pos, cmp_w1, cmp_b1, cmp_w2, cmp_b2, k_norm_b)

    prev_t, diag_t = _band_tiles(tbl)
    bias_a = jnp.concatenate([prev_t[:N_HEADS_A], diag_t[:N_HEADS_A]], axis=-1)
    far_b = tbl[NUM_BUCKETS - 1, N_HEADS_A:].astype(F32)[:, None, None]
    bprev_b = prev_t[N_HEADS_A:] - far_b
    bdiag_b = diag_t[N_HEADS_A:] - far_b
    g2 = _cmp_bias_operand(tbl[:, N_HEADS_A:])

    mix_a = _swa(r3(qa), r3(ka), r3(va), bias_a, sinks_a)
    mix_b = _nsa(r3(qb), kc, vc, r3(ks), r3(vs), r3(kw), r3(vw), r3(gates), bprev_b, bdiag_b, g2)
    y = _ffn(x1, ffn2_norm, ffn2_w_in.astype(BF16), ffn2_w_out.astype(BF16),
             pre=(mix_a.reshape(b * s, QA), mix_b.reshape(b * s, QB), w_mix_out.astype(BF16)))
    return y.reshape(b, s, d)


def kernel(x, rel_bias_table, ffn1_norm, ffn1_w_in, ffn1_w_out, mix_norm, w_mix_in, w_mix_out,
           q_norm_a, k_norm_a, sinks_a, q_norm_b, k_norm_b,
           cmp_pos, cmp_w1, cmp_b1, cmp_w2, cmp_b2, ffn2_norm, ffn2_w_in, ffn2_w_out):
    for l in range(ffn1_norm.shape[0]):
        x = _layer(x, rel_bias_table, ffn1_norm[l], ffn1_w_in[l], ffn1_w_out[l],
                   mix_norm[l], w_mix_in[l], w_mix_out[l],
                   q_norm_a[l], k_norm_a[l], sinks_a[l], q_norm_b[l], k_norm_b[l],
                   cmp_pos[l], cmp_w1[l], cmp_b1[l], cmp_w2[l], cmp_b2[l],
                   ffn2_norm[l], ffn2_w_in[l], ffn2_w_out[l])
    return x
```

```python
import functools
import math

import numpy as np
import jax
import jax.numpy as jnp
from jax import lax
from jax.experimental import pallas as pl
from jax.experimental.pallas import tpu as pltpu

F32 = jnp.float32
BF16 = jnp.bfloat16

D_MODEL = 1024
HEAD_DIM = 64
N_HEADS_A = 8
N_KV_A = 2
N_HEADS_B = 8
N_KV_B = 2
N_HEADS = N_HEADS_A + N_HEADS_B
WINDOW_A = 128
CMP_LEN = 32
CMP_STRIDE = 16
CMP_HIDDEN = 128
SEL_LEN = 64
SEL_TOPK = 16
WINDOW_B = 512
FORCE_BONUS = 1000.0
NUM_BUCKETS = 32
MAX_DISTANCE = 128
D_FF = 2816
EPS = 1e-6
NEG = -1e9

QA = N_HEADS_A * HEAD_DIM
KVA = N_KV_A * HEAD_DIM
QB = N_HEADS_B * HEAD_DIM
KVB = N_KV_B * HEAD_DIM
N_GATES = 3 * N_HEADS_B
OFF_KA = QA
OFF_VA = OFF_KA + KVA
OFF_QB = OFF_VA + KVA
OFF_KVB = OFF_QB + QB
OFF_GB = OFF_KVB + 6 * KVB
D_IN = OFF_GB + N_GATES

LANES = 128
TQ = 128
FAR_TK = 512
SWA_TILES = 4
FFN_TM = 512
FFN_TF = 256
VMEM_LIMIT = 56 * 1024 * 1024

P_QA = 0
P_QB = P_QA + QA
P_KA = P_QB + QB
P_KS = P_KA + LANES
P_KW = P_KS + LANES
P_VA = P_KW + LANES
P_VS = P_VA + LANES
P_VW = P_VS + LANES
P_KVC = P_VW + LANES
P_GATE = P_KVC + 2 * LANES
P_TOTAL = P_GATE + LANES
P_NORM = P_VA


def _nt_dot(a, b):
    return lax.dot_general(a, b, (((1,), (1,)), ((), ())), preferred_element_type=F32)


def _dot(a, b):
    return jnp.dot(a, b, preferred_element_type=F32)


def _rms_rows(x, g):
    ms = jnp.mean(x * x, axis=-1, keepdims=True)
    return x * lax.rsqrt(ms + EPS) * g


def _bucket_np(rel):
    n = np.maximum(rel, 0)
    max_exact = NUM_BUCKETS // 2
    nf = np.maximum(n, 1).astype(np.float64)
    large = max_exact + (np.log(nf / max_exact) / math.log(MAX_DISTANCE / max_exact)
                         * (NUM_BUCKETS - max_exact)).astype(np.int64)
    large = np.minimum(large, NUM_BUCKETS - 1)
    return np.where(n < max_exact, n, large).astype(np.int32)


def _ffn_body(*refs, pre):
    if pre:
        x_ref, ma_ref, mb_ref, wo_ref, nrm_ref, win_ref, wout_ref, o_ref, acc_ref = refs
    else:
        x_ref, nrm_ref, win_ref, wout_ref, o_ref, acc_ref = refs
    x = x_ref[...]
    if pre:
        x = x + _dot(ma_ref[...], wo_ref[0:QA, :]) + _dot(mb_ref[...], wo_ref[QA:QA + QB, :])
    h = _rms_rows(x, nrm_ref[...]).astype(BF16)
    for c in range(D_FF // FFN_TF):
        gate = _dot(h, win_ref[:, c * FFN_TF:(c + 1) * FFN_TF])
        up = _dot(h, win_ref[:, D_FF + c * FFN_TF:D_FF + (c + 1) * FFN_TF])
        a = (gate * jax.nn.sigmoid(gate) * up).astype(BF16)
        part = _dot(a, wout_ref[c * FFN_TF:(c + 1) * FFN_TF, :])
        if c == 0:
            acc_ref[...] = part
        else:
            acc_ref[...] += part
    o_ref[...] = x + 0.5 * acc_ref[...]


def _ffn(x2d, nrm, w_in, w_out, pre=None):
    m = x2d.shape[0]
    tm = FFN_TM
    assert m % tm == 0
    row = lambda i: (i, 0)
    const = lambda i: (0, 0)
    in_specs = [pl.BlockSpec((tm, D_MODEL), row)]
    args = [x2d]
    if pre is not None:
        mix_a, mix_b, wo = pre
        in_specs += [pl.BlockSpec((tm, QA), row), pl.BlockSpec((tm, QB), row),
                     pl.BlockSpec((QA + QB, D_MODEL), const)]
        args += [mix_a, mix_b, wo]
    in_specs += [pl.BlockSpec((1, D_MODEL), const),
                 pl.BlockSpec((D_MODEL, 2 * D_FF), const),
                 pl.BlockSpec((D_FF, D_MODEL), const)]
    args += [nrm.reshape(1, D_MODEL), w_in, w_out]
    return pl.pallas_call(
        functools.partial(_ffn_body, pre=pre is not None),
        out_shape=jax.ShapeDtypeStruct((m, D_MODEL), F32),
        grid=(m // tm,),
        in_specs=in_specs,
        out_specs=pl.BlockSpec((tm, D_MODEL), row),
        scratch_shapes=[pltpu.VMEM((tm, D_MODEL), F32)],
        compiler_params=pltpu.CompilerParams(
            dimension_semantics=("arbitrary",), vmem_limit_bytes=VMEM_LIMIT),
        name="ffn_pre" if pre is not None else "ffn",
    )(*args)


def _proj_body(x_ref, nrm_ref, w_ref, gain_ref, qa_ref, qb_ref, ka_ref, ks_ref, kw_ref,
               va_ref, vs_ref, vw_ref, kvc_ref, gate_ref):
    tm = x_ref.shape[0]
    h = _rms_rows(x_ref[...], nrm_ref[...]).astype(BF16)
    lo = lax.broadcasted_iota(jnp.int32, (tm, LANES), 1) < HEAD_DIM
    inv_d = 1.0 / HEAD_DIM

    def pair(c0):
        p2 = _dot(h, w_ref[:, c0:c0 + 2 * LANES])
        return p2[:, :LANES], p2[:, LANES:]

    def two_heads_f32(p, c0):
        sq = p * p
        ms0 = jnp.sum(jnp.where(lo, sq, 0.0), axis=-1, keepdims=True) * inv_d
        ms1 = jnp.sum(jnp.where(lo, 0.0, sq), axis=-1, keepdims=True) * inv_d
        r = jnp.where(lo, lax.rsqrt(ms0 + EPS), lax.rsqrt(ms1 + EPS))
        return p * r * gain_ref[:, c0:c0 + LANES]

    def two_heads(p, c0):
        return two_heads_f32(p, c0).astype(BF16)

    def spread_query_heads(p, c0, ref, t, n_kv):
        y = two_heads_f32(p, c0)
        swapped = pltpu.roll(y, HEAD_DIM, 1)
        g = (2 * t) // (ref.shape[1] // LANES // n_kv)
        even = jnp.where(lo, y, 0.0) if g == 0 else jnp.where(lo, 0.0, swapped)
        odd = jnp.where(lo, swapped, 0.0) if g == 0 else jnp.where(lo, 0.0, y)
        ref[:, 2 * t * LANES:(2 * t + 1) * LANES] = even.astype(BF16)
        ref[:, (2 * t + 1) * LANES:(2 * t + 2) * LANES] = odd.astype(BF16)

    for ref, base, n_kv in ((qa_ref, P_QA, N_KV_A), (qb_ref, P_QB, N_KV_B)):
        for pc in range(ref.shape[1] // LANES // 4):
            c0 = base + pc * 2 * LANES
            pa, pb = pair(c0)
            spread_query_heads(pa, c0, ref, 2 * pc, n_kv)
            spread_query_heads(pb, c0 + LANES, ref, 2 * pc + 1, n_kv)
    pa, pb = pair(P_KA)
    ka_ref[...] = two_heads(pa, P_KA)
    ks_ref[...] = two_heads(pb, P_KS)
    pa, pb = pair(P_KW)
    kw_ref[...] = two_heads(pa, P_KW)
    va_ref[...] = pb.astype(BF16)
    pa, pb = pair(P_VS)
    vs_ref[...] = pa.astype(BF16)
    vw_ref[...] = pb.astype(BF16)
    pa, pb = pair(P_KVC)
    kvc_ref[:, :LANES] = pa
    kvc_ref[:, LANES:] = pb
    pg = _dot(h, w_ref[:, P_GATE:P_GATE + LANES])
    gate_ref[...] = jax.nn.sigmoid(pg)


def _proj_src_columns():
    src = np.full((P_TOTAL,), -1, np.int64)
    gidx = np.full((P_NORM,), -1, np.int64)
    d = np.arange(HEAD_DIM)
    for hh in range(N_HEADS_A):
        src[P_QA + hh * HEAD_DIM + d] = hh * HEAD_DIM + d
        gidx[P_QA + hh * HEAD_DIM + d] = d
    for hh in range(N_HEADS_B):
        src[P_QB + hh * HEAD_DIM + d] = OFF_QB + hh * HEAD_DIM + d
        gidx[P_QB + hh * HEAD_DIM + d] = HEAD_DIM + d
    two = np.arange(2 * HEAD_DIM)
    src[P_KA + two] = OFF_KA + two
    gidx[P_KA + two] = 2 * HEAD_DIM + two % HEAD_DIM
    src[P_VA + two] = OFF_VA + two
    for slot, dst in ((0, P_KVC), (1, P_KVC + LANES), (2, P_KS), (3, P_VS), (4, P_KW), (5, P_VW)):
        src[dst + two] = OFF_KVB + slot * KVB + two
    gidx[P_KS + two] = 3 * HEAD_DIM + two % HEAD_DIM
    gidx[P_KW + two] = 3 * HEAD_DIM + two % HEAD_DIM
    src[P_GATE + np.arange(N_GATES)] = OFF_GB + np.arange(N_GATES)
    return src, gidx


def _proj(x2d, nrm, w_mix_in, q_norm_a, k_norm_a, q_norm_b, k_norm_b):
    m = x2d.shape[0]
    tm = FFN_TM
    src, gidx = _proj_src_columns()
    pieces, c = [], 0
    while c < P_TOTAL:
        e = c + 1
        while e < P_TOTAL and ((src[e] < 0) == (src[c] < 0)) and (src[c] < 0 or src[e] == src[e - 1] + 1):
            e += 1
        pieces.append(jnp.zeros((D_MODEL, e - c), w_mix_in.dtype) if src[c] < 0
                      else w_mix_in[:, src[c]:src[c] + e - c])
        c = e
    w = jnp.concatenate(pieces, axis=1).astype(BF16)
    scale = HEAD_DIM ** -0.5
    gsrc = jnp.concatenate([q_norm_a * scale, q_norm_b * scale, k_norm_a, k_norm_b]).astype(F32)
    gain = jnp.where(jnp.asarray(gidx >= 0), gsrc[np.maximum(gidx, 0)], 0.0).reshape(1, P_NORM)
    row = lambda i: (i, 0)
    const = lambda i: (0, 0)
    widths = [(N_HEADS_A * LANES, BF16), (N_HEADS_B * LANES, BF16)] + [(LANES, BF16)] * 6 + \
             [(2 * LANES, F32), (LANES, F32)]
    return pl.pallas_call(
        _proj_body,
        out_shape=[jax.ShapeDtypeStruct((m, wd), dt) for wd, dt in widths],
        grid=(m // tm,),
        in_specs=[pl.BlockSpec((tm, D_MODEL), row), pl.BlockSpec((1, D_MODEL), const),
                  pl.BlockSpec((D_MODEL, P_TOTAL), const), pl.BlockSpec((1, P_NORM), const)],
        out_specs=[pl.BlockSpec((tm, wd), row) for wd, _ in widths],
        compiler_params=pltpu.CompilerParams(
            dimension_semantics=("arbitrary",), vmem_limit_bytes=VMEM_LIMIT),
        name="proj",
    )(x2d, nrm.reshape(1, D_MODEL), w, gain)


CMP_FEAT = CMP_STRIDE * 2 * LANES
CMP_SLOTS = 4


def _compress_body(x_ref, pos_ref, w1_ref, b1_ref, w2_ref, b2_ref, gain_ref, kc_ref, vc_ref):
    nck = x_ref.shape[1]
    x = x_ref[0]
    a = _dot((x + pos_ref[0]).astype(BF16), w1_ref[0])
    b = _dot((x + pos_ref[1]).astype(BF16), w1_ref[1])
    pre = a + pltpu.roll(b, nck - 1, 0) + b1_ref[...]
    hid = (pre * jax.nn.sigmoid(pre)).astype(BF16)
    out = _dot(hid, w2_ref[...]) + b2_ref[...]
    k = out[:, :LANES]
    lo = lax.broadcasted_iota(jnp.int32, (nck, LANES), 1) < HEAD_DIM
    sq = k * k
    ms0 = jnp.sum(jnp.where(lo, sq, 0.0), axis=-1, keepdims=True) * (1.0 / HEAD_DIM)
    ms1 = jnp.sum(jnp.where(lo, 0.0, sq), axis=-1, keepdims=True) * (1.0 / HEAD_DIM)
    r = jnp.where(lo, lax.rsqrt(ms0 + EPS), lax.rsqrt(ms1 + EPS))
    kc_ref[0] = (k * r * gain_ref[...]).astype(BF16)
    vc_ref[0] = out[:, LANES:].astype(BF16)


def _compress(kvc, cmp_pos, cmp_w1, cmp_b1, cmp_w2, cmp_b2, k_norm_b):
    b, s, _ = kvc.shape
    nck = s // CMP_STRIDE
    x = kvc.reshape(b, nck, CMP_FEAT)
    r = CMP_LEN // CMP_STRIDE
    w1r = cmp_w1.reshape(2, r, CMP_STRIDE, HEAD_DIM, CMP_HIDDEN)
    eye = jnp.eye(CMP_SLOTS, dtype=F32)
    kv_of_slot = np.array([0, 0, 1, 1])
    w1s = w1r[kv_of_slot]
    w1big = jnp.einsum('sjpdh,st->jpsdth', w1s, eye).reshape(r, CMP_FEAT, CMP_SLOTS * CMP_HIDDEN)
    posr = cmp_pos.reshape(2, r, CMP_STRIDE, HEAD_DIM)[kv_of_slot]
    posbig = posr.transpose(1, 2, 0, 3).reshape(r, 1, CMP_FEAT)
    b1big = cmp_b1[kv_of_slot].reshape(1, CMP_SLOTS * CMP_HIDDEN)
    w2big = jnp.einsum('shd,st->shtd', cmp_w2[kv_of_slot], eye).reshape(
        CMP_SLOTS * CMP_HIDDEN, CMP_SLOTS * HEAD_DIM)
    b2big = cmp_b2[kv_of_slot].reshape(1, CMP_SLOTS * HEAD_DIM)
    gain = jnp.tile(k_norm_b, 2).reshape(1, LANES)
    assert r == 2
    c3 = lambda i: (0, 0, 0)
    c2 = lambda i: (0, 0)
    return pl.pallas_call(
        _compress_body,
        out_shape=[jax.ShapeDtypeStruct((b, nck, LANES), BF16)] * 2,
        grid=(b,),
        in_specs=[pl.BlockSpec((1, nck, CMP_FEAT), lambda i: (i, 0, 0)),
                  pl.BlockSpec((r, 1, CMP_FEAT), c3),
                  pl.BlockSpec((r, CMP_FEAT, CMP_SLOTS * CMP_HIDDEN), c3),
                  pl.BlockSpec((1, CMP_SLOTS * CMP_HIDDEN), c2),
                  pl.BlockSpec((CMP_SLOTS * CMP_HIDDEN, CMP_SLOTS * HEAD_DIM), c2),
                  pl.BlockSpec((1, CMP_SLOTS * HEAD_DIM), c2),
                  pl.BlockSpec((1, LANES), c2)],
        out_specs=[pl.BlockSpec((1, nck, LANES), lambda i: (i, 0, 0))] * 2,
        compiler_params=pltpu.CompilerParams(
            dimension_semantics=("arbitrary",), vmem_limit_bytes=VMEM_LIMIT),
        name="compress",
    )(x, posbig, w1big.astype(BF16), b1big, w2big.astype(BF16), b2big, gain)


def _stack_heads(q_ref, g, heads_per_group):
    return jnp.concatenate(
        [q_ref[0, :, (g * heads_per_group + r) * LANES:(g * heads_per_group + r + 1) * LANES]
         for r in range(heads_per_group)], axis=0)


def _merge_heads(o_heads, g, lane):
    tiles = []
    for c in range(len(o_heads) // 2):
        lo_src = o_heads[2 * c] if g == 0 else pltpu.roll(o_heads[2 * c], HEAD_DIM, 1)
        hi_src = o_heads[2 * c + 1] if g == 1 else pltpu.roll(o_heads[2 * c + 1], HEAD_DIM, 1)
        tiles.append(jnp.where(lane < HEAD_DIM, lo_src, hi_src))
    return tiles


def _band_tiles(tbl):
    i = np.arange(TQ)[:, None]
    j = np.arange(TQ)[None, :]
    return _table_lookup(tbl, _bucket_np(i - j + TQ)), _table_lookup(tbl, _bucket_np(i - j))


def _table_lookup(tbl, buckets):
    onehot = (buckets.reshape(-1)[:, None] == np.arange(NUM_BUCKETS)[None, :]).astype(np.float32)
    out = jnp.einsum('pk,kh->hp', jnp.asarray(onehot), tbl.astype(F32), precision=lax.Precision.HIGHEST)
    return out.reshape((tbl.shape[1],) + buckets.shape)


def _swa_body(sink_ref, q_ref, k_ref, v_ref, bias_ref, o_ref):
    ii = lax.broadcasted_iota(jnp.int32, (TQ, 2 * TQ), 0)
    jj = lax.broadcasted_iota(jnp.int32, (TQ, 2 * TQ), 1)
    lane = lax.broadcasted_iota(jnp.int32, (TQ, LANES), 1)
    rpg = N_HEADS_A // N_KV_A
    for sub in range(SWA_TILES):
        qt = pl.program_id(1) * SWA_TILES + sub
        t0 = qt * TQ
        rows = slice(sub * TQ, (sub + 1) * TQ)
        s_prev = pl.multiple_of(jnp.maximum(t0 - TQ, 0), TQ)
        s_diag = pl.multiple_of(t0, TQ)
        k2 = jnp.concatenate([k_ref[0, pl.ds(s_prev, TQ), :], k_ref[0, pl.ds(s_diag, TQ), :]], axis=0)
        v2 = jnp.concatenate([v_ref[0, pl.ds(s_prev, TQ), :], v_ref[0, pl.ds(s_diag, TQ), :]], axis=0)
        mask = ((jj < TQ) & (jj > ii) & (qt > 0)) | ((jj >= TQ) & ((jj - TQ) <= ii))
        q_all = jnp.concatenate(
            [q_ref[0, rows, hh * LANES:(hh + 1) * LANES] for hh in range(N_HEADS_A)], axis=0)
        s_all = _nt_dot(q_all, k2)
        ps, dens = [], []
        for hh in range(N_HEADS_A):
            s = jnp.where(mask, s_all[hh * TQ:(hh + 1) * TQ] + bias_ref[hh], NEG)
            sink = sink_ref[hh]
            m = jnp.maximum(jnp.max(s, axis=-1, keepdims=True), sink)
            p = jnp.exp(s - m)
            dens.append(jnp.sum(p, axis=-1, keepdims=True) + jnp.exp(sink - m))
            ps.append(p.astype(BF16))
        o_all = _dot(jnp.concatenate(ps, axis=0), v2)
        for g in range(N_KV_A):
            o_heads = [o_all[(g * rpg + r) * TQ:(g * rpg + r + 1) * TQ] / dens[g * rpg + r]
                       for r in range(rpg)]
            for c, tile in enumerate(_merge_heads(o_heads, g, lane)):
                col = (g * rpg // 2 + c) * LANES
                o_ref[0, rows, col:col + LANES] = tile.astype(BF16)


def _swa(qa, ka, va, bias, sinks):
    b, s, _ = qa.shape
    assert WINDOW_A == TQ
    return pl.pallas_call(
        _swa_body,
        out_shape=jax.ShapeDtypeStruct((b, s, QA), BF16),
        grid=(b, s // (SWA_TILES * TQ)),
        in_specs=[pl.BlockSpec(memory_space=pltpu.SMEM),
                  pl.BlockSpec((1, SWA_TILES * TQ, N_HEADS_A * LANES), lambda i, j: (i, j, 0)),
                  pl.BlockSpec((1, s, LANES), lambda i, j: (i, 0, 0)),
                  pl.BlockSpec((1, s, LANES), lambda i, j: (i, 0, 0)),
                  pl.BlockSpec((N_HEADS_A, TQ, 2 * TQ), lambda i, j: (0, 0, 0))],
        out_specs=pl.BlockSpec((1, SWA_TILES * TQ, QA), lambda i, j: (i, j, 0)),
        compiler_params=pltpu.CompilerParams(
            dimension_semantics=("arbitrary", "arbitrary"), vmem_limit_bytes=VMEM_LIMIT),
        name="swa",
    )(sinks.astype(F32), qa, ka, va, bias)


CMP_NEAR = 16
CMP_NEAR_LO = 9


def _nsa_body(q_ref, kc_ref, vc_ref, ks_ref, vs_ref, kw_ref, vw_ref, oh_ref, gate_ref,
              bprev_ref, bdiag_ref, g2_ref, ovl_ref, o_ref, m_ref, acc_ref,
              s0_ref, s1_ref, mc0_ref, mc1_ref, w_ref, *, n_slc, n_top):
    qt = pl.program_id(1)
    t0 = qt * TQ
    rpg = N_HEADS_B // N_KV_B
    rows = rpg * TQ
    ncmp = kc_ref.shape[1]
    nwb = WINDOW_B // TQ
    s_prev = pl.multiple_of(jnp.maximum(t0 - TQ, 0), TQ)
    s_diag = pl.multiple_of(t0, TQ)
    lane = lax.broadcasted_iota(jnp.int32, (TQ, LANES), 1)
    rowi = lax.broadcasted_iota(jnp.int32, (TQ, LANES), 0)
    causal = lane <= rowi
    ones_col = (lane == 0).astype(BF16)
    has_prev = qt > 0

    c_idx = lax.broadcasted_iota(jnp.int32, (ncmp, LANES), 0)
    l_idx = lax.broadcasted_iota(jnp.int32, (ncmp, LANES), 1)
    place = ((c_idx - (TQ // CMP_STRIDE) * qt + CMP_NEAR_LO) == (l_idx & (CMP_NEAR - 1))) \
        & (l_idx < 2 * CMP_NEAR)
    kc_aug = jnp.concatenate([kc_ref[0], place.astype(BF16)], axis=1)
    qi = lax.broadcasted_iota(jnp.int32, (TQ, ncmp), 0)
    ci = lax.broadcasted_iota(jnp.int32, (TQ, ncmp), 1)
    mask_c = (t0 + qi - CMP_STRIDE * ci - (CMP_LEN - 1)) >= 0
    vc = vc_ref[0]

    cur = (t0 + rowi) // SEL_LEN
    valid = (lane <= cur) & (lane < n_slc)
    forced = (lane == 0) | (lane == cur) | (lane == cur - 1)
    cur0 = t0 // SEL_LEN
    near_lo = cur0 - TQ // SEL_LEN

    ks2 = jnp.concatenate(
        [jnp.concatenate([ks_ref[0, pl.ds(s_prev, TQ), :], oh_ref[pl.ds(s_prev, TQ), :]], axis=1),
         jnp.concatenate([ks_ref[0, pl.ds(s_diag, TQ), :], oh_ref[pl.ds(s_diag, TQ), :]], axis=1)],
        axis=0)
    vs2 = jnp.concatenate(
        [jnp.concatenate([vs_ref[0, pl.ds(s_prev, TQ), :], ones_col], axis=1),
         jnp.concatenate([vs_ref[0, pl.ds(s_diag, TQ), :], ones_col], axis=1)], axis=0)
    ones_far = jnp.concatenate([ones_col] * (FAR_TK // TQ), axis=0)

    w_starts = [pl.multiple_of(jnp.maximum(t0 - kb * TQ, 0), TQ) for kb in range(nwb, -1, -1)]
    kw5 = jnp.concatenate([kw_ref[0, pl.ds(st, TQ), :] for st in w_starts], axis=0)
    vw5 = jnp.concatenate([vw_ref[0, pl.ds(st, TQ), :] for st in w_starts], axis=0)

    nh = N_HEADS_B
    q_all = jnp.concatenate([q_ref[0, :, hh * LANES:(hh + 1) * LANES] for hh in range(nh)], axis=0)
    g2_all = jnp.concatenate([g2_ref[g] for g in range(N_KV_B)], axis=0)

    lc_all = _nt_dot(jnp.concatenate([q_all, g2_all], axis=1), kc_aug)
    w_ref[...] = _nt_dot(q_all, kw5)
    pcs = [None] * N_KV_B
    pc_rows = []
    for hh in range(nh):
        lc = jnp.where(mask_c, lc_all[hh * TQ:(hh + 1) * TQ], NEG)
        e = jnp.where(mask_c, jnp.exp(lc - jnp.max(lc, axis=-1, keepdims=True)), 0.0)
        den = jnp.maximum(jnp.sum(e, axis=-1, keepdims=True), jnp.finfo(F32).tiny)
        pc = e / den
        pcs[hh // rpg] = pc if pcs[hh // rpg] is None else pcs[hh // rpg] + pc
        pc_rows.append(pc.astype(BF16))
    o_cmp_all = _dot(jnp.concatenate(pc_rows, axis=0), vc)

    pcs_all = jnp.concatenate(pcs, axis=0)
    pcs_hi = pcs_all.astype(BF16)
    pcs_lo = (pcs_all - pcs_hi.astype(F32)).astype(BF16)
    imp_all = _dot(pcs_hi, ovl_ref[...]) + _dot(pcs_lo, ovl_ref[...])
    s_ts = []
    for g in range(N_KV_B):
        score = jnp.where(valid, imp_all[g * TQ:(g + 1) * TQ] + jnp.where(forced, FORCE_BONUS, 0.0), NEG)
        s_ts.append(score.T[:n_slc])
    s_t = jnp.concatenate(s_ts, axis=1)
    n_io = lax.broadcasted_iota(jnp.int32, (n_slc, N_KV_B * TQ), 0)
    n_acc = 4
    ranks = [jnp.zeros((n_slc, N_KV_B * TQ), F32) for _ in range(n_acc)]
    for mm in range(n_slc):
        sm = s_t[mm:mm + 1, :]
        beats = (sm > s_t) | ((sm == s_t) & (n_io > mm))
        ranks[mm % n_acc] = ranks[mm % n_acc] + jnp.where(beats, 1.0, 0.0)
    rank = (ranks[0] + ranks[1]) + (ranks[2] + ranks[3])
    sel_t = jnp.where((rank < n_top) & (s_t > 0.5 * NEG), 1.0, 0.0)
    m_nears, m_fars = [], []
    for g in range(N_KV_B):
        st = sel_t[:, g * TQ:(g + 1) * TQ]
        if n_slc < LANES:
            st = jnp.concatenate([st, jnp.zeros((LANES - n_slc, TQ), F32)], axis=0)
        sel = st.T > 0.5
        m_nears.append(jnp.where(sel & (lane >= near_lo), 0.0, NEG).astype(BF16))
        m_fars.append(jnp.where(sel & (lane < near_lo), 0.0, NEG).astype(BF16))
    q_near = jnp.concatenate(
        [q_all, jnp.concatenate([m_nears[hh // rpg] for hh in range(nh)], axis=0)], axis=1)
    q_far = jnp.concatenate(
        [q_all, jnp.concatenate([m_fars[hh // rpg] for hh in range(nh)], axis=0)], axis=1)

    s_all = _nt_dot(q_near, ks2)
    s_rows = []
    for hh in range(nh):
        sr = s_all[hh * TQ:(hh + 1) * TQ]
        s_p = jnp.where(has_prev, sr[:, :TQ] + bprev_ref[hh], NEG)
        s_d = jnp.where(causal, sr[:, TQ:] + bdiag_ref[hh], NEG)
        s_rows.append(jnp.concatenate([s_p, s_d], axis=1))
    s_near = jnp.concatenate(s_rows, axis=0)
    m0 = jnp.max(s_near, axis=-1, keepdims=True)
    m_ref[...] = jnp.broadcast_to(m0, (nh * TQ, LANES))
    acc_ref[...] = _dot(jnp.exp(s_near - m0).astype(BF16), vs2)

    n_far = (qt + FAR_TK // TQ - 2) // (FAR_TK // TQ)
    last_chunk = ks_ref.shape[1] // FAR_TK - 1
    s_bufs = (s0_ref, s1_ref)
    mc_bufs = (mc0_ref, mc1_ref)

    def chunk_start(c):
        return pl.multiple_of(jnp.minimum(c, last_chunk) * FAR_TK, FAR_TK)

    def produce(slot, c):
        k0 = chunk_start(c)
        kk = jnp.concatenate([ks_ref[0, pl.ds(k0, FAR_TK), :], oh_ref[pl.ds(k0, FAR_TK), :]], axis=1)
        s = _nt_dot(q_far, kk)
        s_bufs[slot][...] = s
        mc_bufs[slot][...] = jnp.broadcast_to(jnp.max(s, axis=-1, keepdims=True), (s.shape[0], LANES))

    def consume(slot, c):
        k0 = chunk_start(c)
        vv = jnp.concatenate([vs_ref[0, pl.ds(k0, FAR_TK), :], ones_far], axis=1)
        m_old = m_ref[...]
        m_new = jnp.maximum(m_old, mc_bufs[slot][...])
        alpha = jnp.exp(m_old - m_new)
        p = jnp.exp(s_bufs[slot][...] - jnp.concatenate([m_new] * (FAR_TK // LANES), axis=1)).astype(BF16)
        acc_ref[...] = acc_ref[...] * jnp.concatenate([alpha] * 2, axis=1) + _dot(p, vv)
        m_ref[...] = m_new

    produce(0, 0)

    def far_pair(j, carry):
        produce(1, 2 * j + 1)
        consume(0, 2 * j)
        produce(0, 2 * j + 2)
        consume(1, 2 * j + 1)
        return carry

    lax.fori_loop(0, (n_far + 1) // 2, far_pair, 0)
    acc = acc_ref[...]
    o_sel_all = acc[:, :LANES] / acc[:, LANES:LANES + 1]

    pw, lw = [], []
    for hh in range(nh):
        wr = w_ref[hh * TQ:(hh + 1) * TQ, :]
        blocks = []
        for bi, kb in enumerate(range(nwb, -1, -1)):
            blk = wr[:, bi * TQ:(bi + 1) * TQ]
            if kb == nwb:
                blk = jnp.where((lane > rowi) & (qt >= kb), blk, NEG)
            elif kb >= 2:
                blk = jnp.where(qt >= kb, blk, NEG)
            elif kb == 1:
                blk = jnp.where(has_prev, blk + bprev_ref[hh], NEG)
            else:
                blk = jnp.where(causal, blk + bdiag_ref[hh], NEG)
            blocks.append(blk)
        sw = jnp.concatenate(blocks, axis=1)
        p = jnp.exp(sw - jnp.max(sw, axis=-1, keepdims=True))
        lw.append(jnp.sum(p, axis=-1, keepdims=True))
        pw.append(p.astype(BF16))
    o_win_all = _dot(jnp.concatenate(pw, axis=0), vw5)

    for g in range(N_KV_B):
        o_heads = []
        for r in range(rpg):
            hh = g * rpg + r
            sl = slice(hh * TQ, (hh + 1) * TQ)
            g_cmp = gate_ref[0, :, hh:hh + 1]
            g_sel = gate_ref[0, :, N_HEADS_B + hh:N_HEADS_B + hh + 1]
            g_win = gate_ref[0, :, 2 * N_HEADS_B + hh:2 * N_HEADS_B + hh + 1]
            o_heads.append(g_cmp * o_cmp_all[sl] + g_sel * o_sel_all[sl] + g_win * (o_win_all[sl] / lw[hh]))
        for c, tile in enumerate(_merge_heads(o_heads, g, lane)):
            col = (g * rpg // 2 + c) * LANES
            o_ref[0, :, col:col + LANES] = tile.astype(BF16)


def _nsa(qb, kc, vc, ks, vs, kw, vw, gates, bprev, bdiag, g2):
    b, s, _ = qb.shape
    assert s % (2 * FAR_TK) == 0 and WINDOW_B % TQ == 0 and TQ % SEL_LEN == 0
    n_slc = s // SEL_LEN
    assert n_slc <= LANES and n_slc % 8 == 0
    n_top = min(SEL_TOPK, n_slc)
    ncmp = s // CMP_STRIDE
    rpg = N_HEADS_B // N_KV_B
    kpos = np.arange(s)
    onehot = (kpos[:, None] // SEL_LEN == np.arange(LANES)[None, :]).astype(np.float32)
    cstart = np.arange(ncmp) * CMP_STRIDE
    sstart = np.arange(LANES) * SEL_LEN
    ovl = ((cstart[:, None] < sstart[None, :] + SEL_LEN)
           & (cstart[:, None] + CMP_LEN - 1 >= sstart[None, :])
           & (np.arange(ncmp)[:, None] < ncmp - CMP_LEN // CMP_STRIDE + 1)
           & (np.arange(LANES)[None, :] < n_slc)).astype(np.float32)
    full = lambda i, j: (i, 0, 0)
    return pl.pallas_call(
        functools.partial(_nsa_body, n_slc=n_slc, n_top=n_top),
        out_shape=jax.ShapeDtypeStruct((b, s, QB), BF16),
        grid=(b, s // TQ),
        in_specs=[pl.BlockSpec((1, TQ, N_HEADS_B * LANES), lambda i, j: (i, j, 0)),
                  pl.BlockSpec((1, ncmp, LANES), full), pl.BlockSpec((1, ncmp, LANES), full),
                  pl.BlockSpec((1, s, LANES), full), pl.BlockSpec((1, s, LANES), full),
                  pl.BlockSpec((1, s, LANES), full), pl.BlockSpec((1, s, LANES), full),
                  pl.BlockSpec((s, LANES), lambda i, j: (0, 0)),
                  pl.BlockSpec((1, TQ, LANES), lambda i, j: (i, j, 0)),
                  pl.BlockSpec((N_HEADS_B, TQ, TQ), lambda i, j: (0, 0, 0)),
                  pl.BlockSpec((N_HEADS_B, TQ, TQ), lambda i, j: (0, 0, 0)),
                  pl.BlockSpec((N_KV_B, rpg * TQ, LANES), lambda i, j: (0, 0, 0)),
                  pl.BlockSpec((ncmp, LANES), lambda i, j: (0, 0))],
        out_specs=pl.BlockSpec((1, TQ, QB), lambda i, j: (i, j, 0)),
        scratch_shapes=[pltpu.VMEM((N_KV_B * rpg * TQ, LANES), F32),
                        pltpu.VMEM((N_KV_B * rpg * TQ, 2 * LANES), F32),
                        pltpu.VMEM((N_KV_B * rpg * TQ, FAR_TK), F32),
                        pltpu.VMEM((N_KV_B * rpg * TQ, FAR_TK), F32),
                        pltpu.VMEM((N_KV_B * rpg * TQ, LANES), F32),
                        pltpu.VMEM((N_KV_B * rpg * TQ, LANES), F32),
                        pltpu.VMEM((N_KV_B * rpg * TQ, WINDOW_B + TQ), F32)],
        compiler_params=pltpu.CompilerParams(
            dimension_semantics=("arbitrary", "arbitrary"), vmem_limit_bytes=VMEM_LIMIT),
        name="nsa",
    )(qb, kc, vc, ks, vs, kw, vw, jnp.asarray(onehot, BF16), gates, bprev, bdiag, g2,
      jnp.asarray(ovl, BF16))


def _cmp_bias_operand(tbl_b):
    i = np.arange(TQ)[:, None]
    u = np.arange(CMP_NEAR)[None, :]
    rel = i - CMP_STRIDE * (u - CMP_NEAR_LO) - (CMP_LEN - 1)
    assert rel[0, 0] + CMP_STRIDE >= MAX_DISTANCE and rel[-1, -1] - CMP_STRIDE < 0
    near = (rel >= 0) & (rel < MAX_DISTANCE)
    far_const = tbl_b[NUM_BUCKETS - 1]
    vals = jnp.where(jnp.asarray(near)[None], _table_lookup(tbl_b, _bucket_np(rel))
                     - far_const[:, None, None], 0.0)
    hi = vals.astype(BF16)
    lo = (vals - hi.astype(F32)).astype(BF16)
    pad = jnp.zeros((N_HEADS_B, TQ, LANES - 2 * CMP_NEAR), BF16)
    out = jnp.concatenate([hi, lo, pad], axis=-1)
    rpg = N_HEADS_B // N_KV_B
    return out.reshape(N_KV_B, rpg * TQ, LANES)


def _layer(x, tbl, ffn1_norm, ffn1_w_in, ffn1_w_out, mix_norm, w_mix_in, w_mix_out,
           q_norm_a, k_norm_a, sinks_a, q_norm_b, k_norm_b,
           cmp_pos, cmp_w1, cmp_b1, cmp_w2, cmp_b2, ffn2_norm, ffn2_w_in, ffn2_w_out):
    b, s, d = x.shape
    x2d = x.reshape(b * s, d)
    x1 = _ffn(x2d, ffn1_norm, ffn1_w_in.astype(BF16), ffn1_w_out.astype(BF16))
    qa, qb, ka, ks, kw, va, vs, vw, kvc, gates = _proj(
        x1, mix_norm, w_mix_in, q_norm_a, k_norm_a, q_norm_b, k_norm_b)
    r3 = lambda a: a.reshape(b, s, a.shape[-1])
    kc, vc = _compress(r3(kvc), cmp_pos, cmp_w1, cmp_b1, cmp_w2, cmp_b2, k_norm_b)

    prev_t, diag_t = _band_tiles(tbl)
    bias_a = jnp.concatenate([prev_t[:N_HEADS_A], diag_t[:N_HEADS_A]], axis=-1)
    far_b = tbl[NUM_BUCKETS - 1, N_HEADS_A:].astype(F32)[:, None, None]
    bprev_b = prev_t[N_HEADS_A:] - far_b
    bdiag_b = diag_t[N_HEADS_A:] - far_b
    g2 = _cmp_bias_operand(tbl[:, N_HEADS_A:])

    mix_a = _swa(r3(qa), r3(ka), r3(va), bias_a, sinks_a)
    mix_b = _nsa(r3(qb), kc, vc, r3(ks), r3(vs), r3(kw), r3(vw), r3(gates), bprev_b, bdiag_b, g2)
    y = _ffn(x1, ffn2_norm, ffn2_w_in.astype(BF16), ffn2_w_out.astype(BF16),
             pre=(mix_a.reshape(b * s, QA), mix_b.reshape(b * s, QB), w_mix_out.astype(BF16)))
    return y.reshape(b, s, d)


def kernel(x, rel_bias_table, ffn1_norm, ffn1_w_in, ffn1_w_out, mix_norm, w_mix_in, w_mix_out,
           q_norm_a, k_norm_a, sinks_a, q_norm_b, k_norm_b,
           cmp_pos, cmp_w1, cmp_b1, cmp_w2, cmp_b2, ffn2_norm, ffn2_w_in, ffn2_w_out):
    for l in range(ffn1_norm.shape[0]):
        x = _layer(x, rel_bias_table, ffn1_norm[l], ffn1_w_in[l], ffn1_w_out[l],
                   mix_norm[l], w_mix_in[l], w_mix_out[l],
                   q_norm_a[l], k_norm_a[l], sinks_a[l], q_norm_b[l], k_norm_b[l],
                   cmp_pos[l], cmp_w1[l], cmp_b1[l], cmp_w2[l], cmp_b2[l],
                   ffn2_norm[l], ffn2_w_in[l], ffn2_w_out[l])
    return x
```

```python
import functools
import math

import numpy as np
import jax
import jax.numpy as jnp
from jax import lax
from jax.experimental import pallas as pl
from jax.experimental.pallas import tpu as pltpu

F32 = jnp.float32
BF16 = jnp.bfloat16

D_MODEL = 1024
HEAD_DIM = 64
N_HEADS_A = 8
N_KV_A = 2
N_HEADS_B = 8
N_KV_B = 2
N_HEADS = N_HEADS_A + N_HEADS_B
WINDOW_A = 128
CMP_LEN = 32
CMP_STRIDE = 16
CMP_HIDDEN = 128
SEL_LEN = 64
SEL_TOPK = 16
WINDOW_B = 512
FORCE_BONUS = 1000.0
NUM_BUCKETS = 32
MAX_DISTANCE = 128
D_FF = 2816
EPS = 1e-6
NEG = -1e9

QA = N_HEADS_A * HEAD_DIM
KVA = N_KV_A * HEAD_DIM
QB = N_HEADS_B * HEAD_DIM
KVB = N_KV_B * HEAD_DIM
N_GATES = 3 * N_HEADS_B
OFF_KA = QA
OFF_VA = OFF_KA + KVA
OFF_QB = OFF_VA + KVA
OFF_KVB = OFF_QB + QB
OFF_GB = OFF_KVB + 6 * KVB
D_IN = OFF_GB + N_GATES

LANES = 128
TQ = 128
FAR_TK = 512
NSA_TILES = 2
SWA_TILES = 4
FFN_TM = 512
FFN_TF = 256
VMEM_LIMIT = 56 * 1024 * 1024

P_QA = 0
P_QB = P_QA + QA
P_KA = P_QB + QB
P_KS = P_KA + LANES
P_KW = P_KS + LANES
P_VA = P_KW + LANES
P_VS = P_VA + LANES
P_VW = P_VS + LANES
P_KVC = P_VW + LANES
P_GATE = P_KVC + 2 * LANES
P_TOTAL = P_GATE + LANES
P_NORM = P_VA


def _nt_dot(a, b):
    return lax.dot_general(a, b, (((1,), (1,)), ((), ())), preferred_element_type=F32)


def _dot(a, b):
    return jnp.dot(a, b, preferred_element_type=F32)


def _rms_rows(x, g):
    ms = jnp.mean(x * x, axis=-1, keepdims=True)
    return x * lax.rsqrt(ms + EPS) * g


def _bucket_np(rel):
    n = np.maximum(rel, 0)
    max_exact = NUM_BUCKETS // 2
    nf = np.maximum(n, 1).astype(np.float64)
    large = max_exact + (np.log(nf / max_exact) / math.log(MAX_DISTANCE / max_exact)
                         * (NUM_BUCKETS - max_exact)).astype(np.int64)
    large = np.minimum(large, NUM_BUCKETS - 1)
    return np.where(n < max_exact, n, large).astype(np.int32)


def _ffn_body(*refs, pre):
    if pre:
        x_ref, ma_ref, mb_ref, wo_ref, nrm_ref, win_ref, wout_ref, o_ref, acc_ref = refs
    else:
        x_ref, nrm_ref, win_ref, wout_ref, o_ref, acc_ref = refs
    x = x_ref[...]
    if pre:
        x = x + _dot(ma_ref[...], wo_ref[0:QA, :]) + _dot(mb_ref[...], wo_ref[QA:QA + QB, :])
    h = _rms_rows(x, nrm_ref[...]).astype(BF16)
    for c in range(D_FF // FFN_TF):
        gate = _dot(h, win_ref[:, c * FFN_TF:(c + 1) * FFN_TF])
        up = _dot(h, win_ref[:, D_FF + c * FFN_TF:D_FF + (c + 1) * FFN_TF])
        a = (gate * jax.nn.sigmoid(gate) * up).astype(BF16)
        part = _dot(a, wout_ref[c * FFN_TF:(c + 1) * FFN_TF, :])
        if c == 0:
            acc_ref[...] = part
        else:
            acc_ref[...] += part
    o_ref[...] = x + 0.5 * acc_ref[...]


def _ffn(x2d, nrm, w_in, w_out, pre=None):
    m = x2d.shape[0]
    tm = FFN_TM
    assert m % tm == 0
    row = lambda i: (i, 0)
    const = lambda i: (0, 0)
    in_specs = [pl.BlockSpec((tm, D_MODEL), row)]
    args = [x2d]
    if pre is not None:
        mix_a, mix_b, wo = pre
        in_specs += [pl.BlockSpec((tm, QA), row), pl.BlockSpec((tm, QB), row),
                     pl.BlockSpec((QA + QB, D_MODEL), const)]
        args += [mix_a, mix_b, wo]
    in_specs += [pl.BlockSpec((1, D_MODEL), const),
                 pl.BlockSpec((D_MODEL, 2 * D_FF), const),
                 pl.BlockSpec((D_FF, D_MODEL), const)]
    args += [nrm.reshape(1, D_MODEL), w_in, w_out]
    return pl.pallas_call(
        functools.partial(_ffn_body, pre=pre is not None),
        out_shape=jax.ShapeDtypeStruct((m, D_MODEL), F32),
        grid=(m // tm,),
        in_specs=in_specs,
        out_specs=pl.BlockSpec((tm, D_MODEL), row),
        scratch_shapes=[pltpu.VMEM((tm, D_MODEL), F32)],
        compiler_params=pltpu.CompilerParams(
            dimension_semantics=("arbitrary",), vmem_limit_bytes=VMEM_LIMIT),
        name="ffn_pre" if pre is not None else "ffn",
    )(*args)


def _proj_body(x_ref, nrm_ref, w_ref, gain_ref, qa_ref, qb_ref, ka_ref, ks_ref, kw_ref,
               va_ref, vs_ref, vw_ref, kvc_ref, gate_ref):
    tm = x_ref.shape[0]
    h = _rms_rows(x_ref[...], nrm_ref[...]).astype(BF16)
    lo = lax.broadcasted_iota(jnp.int32, (tm, LANES), 1) < HEAD_DIM
    inv_d = 1.0 / HEAD_DIM

    def pair(c0):
        p2 = _dot(h, w_ref[:, c0:c0 + 2 * LANES])
        return p2[:, :LANES], p2[:, LANES:]

    def two_heads_f32(p, c0):
        sq = p * p
        ms0 = jnp.sum(jnp.where(lo, sq, 0.0), axis=-1, keepdims=True) * inv_d
        ms1 = jnp.sum(jnp.where(lo, 0.0, sq), axis=-1, keepdims=True) * inv_d
        r = jnp.where(lo, lax.rsqrt(ms0 + EPS), lax.rsqrt(ms1 + EPS))
        return p * r * gain_ref[:, c0:c0 + LANES]

    def two_heads(p, c0):
        return two_heads_f32(p, c0).astype(BF16)

    def spread_query_heads(p, c0, ref, t, n_kv):
        y = two_heads_f32(p, c0)
        swapped = pltpu.roll(y, HEAD_DIM, 1)
        g = (2 * t) // (ref.shape[1] // LANES // n_kv)
        even = jnp.where(lo, y, 0.0) if g == 0 else jnp.where(lo, 0.0, swapped)
        odd = jnp.where(lo, swapped, 0.0) if g == 0 else jnp.where(lo, 0.0, y)
        ref[:, 2 * t * LANES:(2 * t + 1) * LANES] = even.astype(BF16)
        ref[:, (2 * t + 1) * LANES:(2 * t + 2) * LANES] = odd.astype(BF16)

    for ref, base, n_kv in ((qa_ref, P_QA, N_KV_A), (qb_ref, P_QB, N_KV_B)):
        for pc in range(ref.shape[1] // LANES // 4):
            c0 = base + pc * 2 * LANES
            pa, pb = pair(c0)
            spread_query_heads(pa, c0, ref, 2 * pc, n_kv)
            spread_query_heads(pb, c0 + LANES, ref, 2 * pc + 1, n_kv)
    pa, pb = pair(P_KA)
    ka_ref[...] = two_heads(pa, P_KA)
    ks_ref[...] = two_heads(pb, P_KS)
    pa, pb = pair(P_KW)
    kw_ref[...] = two_heads(pa, P_KW)
    va_ref[...] = pb.astype(BF16)
    pa, pb = pair(P_VS)
    vs_ref[...] = pa.astype(BF16)
    vw_ref[...] = pb.astype(BF16)
    pa, pb = pair(P_KVC)
    kvc_ref[:, :LANES] = pa
    kvc_ref[:, LANES:] = pb
    pg = _dot(h, w_ref[:, P_GATE:P_GATE + LANES])
    gate_ref[...] = jax.nn.sigmoid(pg)


def _proj_src_columns():
    src = np.full((P_TOTAL,), -1, np.int64)
    gidx = np.full((P_NORM,), -1, np.int64)
    d = np.arange(HEAD_DIM)
    for hh in range(N_HEADS_A):
        src[P_QA + hh * HEAD_DIM + d] = hh * HEAD_DIM + d
        gidx[P_QA + hh * HEAD_DIM + d] = d
    for hh in range(N_HEADS_B):
        src[P_QB + hh * HEAD_DIM + d] = OFF_QB + hh * HEAD_DIM + d
        gidx[P_QB + hh * HEAD_DIM + d] = HEAD_DIM + d
    two = np.arange(2 * HEAD_DIM)
    src[P_KA + two] = OFF_KA + two
    gidx[P_KA + two] = 2 * HEAD_DIM + two % HEAD_DIM
    src[P_VA + two] = OFF_VA + two
    for slot, dst in ((0, P_KVC), (1, P_KVC + LANES), (2, P_KS), (3, P_VS), (4, P_KW), (5, P_VW)):
        src[dst + two] = OFF_KVB + slot * KVB + two
    gidx[P_KS + two] = 3 * HEAD_DIM + two % HEAD_DIM
    gidx[P_KW + two] = 3 * HEAD_DIM + two % HEAD_DIM
    src[P_GATE + np.arange(N_GATES)] = OFF_GB + np.arange(N_GATES)
    return src, gidx


def _proj(x2d, nrm, w_mix_in, q_norm_a, k_norm_a, q_norm_b, k_norm_b):
    m = x2d.shape[0]
    tm = FFN_TM
    src, gidx = _proj_src_columns()
    pieces, c = [], 0
    while c < P_TOTAL:
        e = c + 1
        while e < P_TOTAL and ((src[e] < 0) == (src[c] < 0)) and (src[c] < 0 or src[e] == src[e - 1] + 1):
            e += 1
        pieces.append(jnp.zeros((D_MODEL, e - c), w_mix_in.dtype) if src[c] < 0
                      else w_mix_in[:, src[c]:src[c] + e - c])
        c = e
    w = jnp.concatenate(pieces, axis=1).astype(BF16)
    scale = HEAD_DIM ** -0.5
    gsrc = jnp.concatenate([q_norm_a * scale, q_norm_b * scale, k_norm_a, k_norm_b]).astype(F32)
    gain = jnp.where(jnp.asarray(gidx >= 0), gsrc[np.maximum(gidx, 0)], 0.0).reshape(1, P_NORM)
    row = lambda i: (i, 0)
    const = lambda i: (0, 0)
    widths = [(N_HEADS_A * LANES, BF16), (N_HEADS_B * LANES, BF16)] + [(LANES, BF16)] * 6 + \
             [(2 * LANES, F32), (LANES, F32)]
    return pl.pallas_call(
        _proj_body,
        out_shape=[jax.ShapeDtypeStruct((m, wd), dt) for wd, dt in widths],
        grid=(m // tm,),
        in_specs=[pl.BlockSpec((tm, D_MODEL), row), pl.BlockSpec((1, D_MODEL), const),
                  pl.BlockSpec((D_MODEL, P_TOTAL), const), pl.BlockSpec((1, P_NORM), const)],
        out_specs=[pl.BlockSpec((tm, wd), row) for wd, _ in widths],
        compiler_params=pltpu.CompilerParams(
            dimension_semantics=("arbitrary",), vmem_limit_bytes=VMEM_LIMIT),
        name="proj",
    )(x2d, nrm.reshape(1, D_MODEL), w, gain)


CMP_FEAT = CMP_STRIDE * 2 * LANES
CMP_SLOTS = 4


def _compress_body(x_ref, pos_ref, w1_ref, b1_ref, w2_ref, b2_ref, gain_ref, kc_ref, vc_ref):
    nck = x_ref.shape[1]
    x = x_ref[0]
    a = _dot((x + pos_ref[0]).astype(BF16), w1_ref[0])
    b = _dot((x + pos_ref[1]).astype(BF16), w1_ref[1])
    pre = a + pltpu.roll(b, nck - 1, 0) + b1_ref[...]
    hid = (pre * jax.nn.sigmoid(pre)).astype(BF16)
    out = _dot(hid, w2_ref[...]) + b2_ref[...]
    k = out[:, :LANES]
    lo = lax.broadcasted_iota(jnp.int32, (nck, LANES), 1) < HEAD_DIM
    sq = k * k
    ms0 = jnp.sum(jnp.where(lo, sq, 0.0), axis=-1, keepdims=True) * (1.0 / HEAD_DIM)
    ms1 = jnp.sum(jnp.where(lo, 0.0, sq), axis=-1, keepdims=True) * (1.0 / HEAD_DIM)
    r = jnp.where(lo, lax.rsqrt(ms0 + EPS), lax.rsqrt(ms1 + EPS))
    kc_ref[0] = (k * r * gain_ref[...]).astype(BF16)
    vc_ref[0] = out[:, LANES:].astype(BF16)


def _compress(kvc, cmp_pos, cmp_w1, cmp_b1, cmp_w2, cmp_b2, k_norm_b):
    b, s, _ = kvc.shape
    nck = s // CMP_STRIDE
    x = kvc.reshape(b, nck, CMP_FEAT)
    r = CMP_LEN // CMP_STRIDE
    w1r = cmp_w1.reshape(2, r, CMP_STRIDE, HEAD_DIM, CMP_HIDDEN)
    eye = jnp.eye(CMP_SLOTS, dtype=F32)
    kv_of_slot = np.array([0, 0, 1, 1])
    w1s = w1r[kv_of_slot]
    w1big = jnp.einsum('sjpdh,st->jpsdth', w1s, eye).reshape(r, CMP_FEAT, CMP_SLOTS * CMP_HIDDEN)
    posr = cmp_pos.reshape(2, r, CMP_STRIDE, HEAD_DIM)[kv_of_slot]
    posbig = posr.transpose(1, 2, 0, 3).reshape(r, 1, CMP_FEAT)
    b1big = cmp_b1[kv_of_slot].reshape(1, CMP_SLOTS * CMP_HIDDEN)
    w2big = jnp.einsum('shd,st->shtd', cmp_w2[kv_of_slot], eye).reshape(
        CMP_SLOTS * CMP_HIDDEN, CMP_SLOTS * HEAD_DIM)
    b2big = cmp_b2[kv_of_slot].reshape(1, CMP_SLOTS * HEAD_DIM)
    gain = jnp.tile(k_norm_b, 2).reshape(1, LANES)
    assert r == 2
    c3 = lambda i: (0, 0, 0)
    c2 = lambda i: (0, 0)
    return pl.pallas_call(
        _compress_body,
        out_shape=[jax.ShapeDtypeStruct((b, nck, LANES), BF16)] * 2,
        grid=(b,),
        in_specs=[pl.BlockSpec((1, nck, CMP_FEAT), lambda i: (i, 0, 0)),
                  pl.BlockSpec((r, 1, CMP_FEAT), c3),
                  pl.BlockSpec((r, CMP_FEAT, CMP_SLOTS * CMP_HIDDEN), c3),
                  pl.BlockSpec((1, CMP_SLOTS * CMP_HIDDEN), c2),
                  pl.BlockSpec((CMP_SLOTS * CMP_HIDDEN, CMP_SLOTS * HEAD_DIM), c2),
                  pl.BlockSpec((1, CMP_SLOTS * HEAD_DIM), c2),
                  pl.BlockSpec((1, LANES), c2)],
        out_specs=[pl.BlockSpec((1, nck, LANES), lambda i: (i, 0, 0))] * 2,
        compiler_params=pltpu.CompilerParams(
            dimension_semantics=("arbitrary",), vmem_limit_bytes=VMEM_LIMIT),
        name="compress",
    )(x, posbig, w1big.astype(BF16), b1big, w2big.astype(BF16), b2big, gain)


def _stack_heads(q_ref, g, heads_per_group):
    return jnp.concatenate(
        [q_ref[0, :, (g * heads_per_group + r) * LANES:(g * heads_per_group + r + 1) * LANES]
         for r in range(heads_per_group)], axis=0)


def _merge_heads(o_heads, g, lane):
    tiles = []
    for c in range(len(o_heads) // 2):
        lo_src = o_heads[2 * c] if g == 0 else pltpu.roll(o_heads[2 * c], HEAD_DIM, 1)
        hi_src = o_heads[2 * c + 1] if g == 1 else pltpu.roll(o_heads[2 * c + 1], HEAD_DIM, 1)
        tiles.append(jnp.where(lane < HEAD_DIM, lo_src, hi_src))
    return tiles


def _band_tiles(tbl):
    i = np.arange(TQ)[:, None]
    j = np.arange(TQ)[None, :]
    return _table_lookup(tbl, _bucket_np(i - j + TQ)), _table_lookup(tbl, _bucket_np(i - j))


def _table_lookup(tbl, buckets):
    onehot = (buckets.reshape(-1)[:, None] == np.arange(NUM_BUCKETS)[None, :]).astype(np.float32)
    out = jnp.einsum('pk,kh->hp', jnp.asarray(onehot), tbl.astype(F32), precision=lax.Precision.HIGHEST)
    return out.reshape((tbl.shape[1],) + buckets.shape)


def _swa_body(sink_ref, q_ref, k_ref, v_ref, bias_ref, o_ref):
    ii = lax.broadcasted_iota(jnp.int32, (TQ, 2 * TQ), 0)
    jj = lax.broadcasted_iota(jnp.int32, (TQ, 2 * TQ), 1)
    lane = lax.broadcasted_iota(jnp.int32, (TQ, LANES), 1)
    rpg = N_HEADS_A // N_KV_A
    for sub in range(SWA_TILES):
        qt = pl.program_id(1) * SWA_TILES + sub
        t0 = qt * TQ
        rows = slice(sub * TQ, (sub + 1) * TQ)
        s_prev = pl.multiple_of(jnp.maximum(t0 - TQ, 0), TQ)
        s_diag = pl.multiple_of(t0, TQ)
        k2 = jnp.concatenate([k_ref[0, pl.ds(s_prev, TQ), :], k_ref[0, pl.ds(s_diag, TQ), :]], axis=0)
        v2 = jnp.concatenate([v_ref[0, pl.ds(s_prev, TQ), :], v_ref[0, pl.ds(s_diag, TQ), :]], axis=0)
        mask = ((jj < TQ) & (jj > ii) & (qt > 0)) | ((jj >= TQ) & ((jj - TQ) <= ii))
        q_all = jnp.concatenate(
            [q_ref[0, rows, hh * LANES:(hh + 1) * LANES] for hh in range(N_HEADS_A)], axis=0)
        s_all = _nt_dot(q_all, k2)
        ps, dens = [], []
        for hh in range(N_HEADS_A):
            s = jnp.where(mask, s_all[hh * TQ:(hh + 1) * TQ] + bias_ref[hh], NEG)
            sink = sink_ref[hh]
            m = jnp.maximum(jnp.max(s, axis=-1, keepdims=True), sink)
            p = jnp.exp(s - m)
            dens.append(jnp.sum(p, axis=-1, keepdims=True) + jnp.exp(sink - m))
            ps.append(p.astype(BF16))
        o_all = _dot(jnp.concatenate(ps, axis=0), v2)
        for g in range(N_KV_A):
            o_heads = [o_all[(g * rpg + r) * TQ:(g * rpg + r + 1) * TQ] / dens[g * rpg + r]
                       for r in range(rpg)]
            for c, tile in enumerate(_merge_heads(o_heads, g, lane)):
                col = (g * rpg // 2 + c) * LANES
                o_ref[0, rows, col:col + LANES] = tile.astype(BF16)


def _swa(qa, ka, va, bias, sinks):
    b, s, _ = qa.shape
    assert WINDOW_A == TQ
    return pl.pallas_call(
        _swa_body,
        out_shape=jax.ShapeDtypeStruct((b, s, QA), BF16),
        grid=(b, s // (SWA_TILES * TQ)),
        in_specs=[pl.BlockSpec(memory_space=pltpu.SMEM),
                  pl.BlockSpec((1, SWA_TILES * TQ, N_HEADS_A * LANES), lambda i, j: (i, j, 0)),
                  pl.BlockSpec((1, s, LANES), lambda i, j: (i, 0, 0)),
                  pl.BlockSpec((1, s, LANES), lambda i, j: (i, 0, 0)),
                  pl.BlockSpec((N_HEADS_A, TQ, 2 * TQ), lambda i, j: (0, 0, 0))],
        out_specs=pl.BlockSpec((1, SWA_TILES * TQ, QA), lambda i, j: (i, j, 0)),
        compiler_params=pltpu.CompilerParams(
            dimension_semantics=("arbitrary", "arbitrary"), vmem_limit_bytes=VMEM_LIMIT),
        name="swa",
    )(sinks.astype(F32), qa, ka, va, bias)


CMP_NEAR = 16
CMP_NEAR_LO = 9


def _nsa_select(sub, q_ref, kc_ref, vc_ref, ks_ref, vs_ref, kw_ref, vw_ref, oh_ref,
                bprev_ref, bdiag_ref, g2_ref, ovl_ref, m_ref, acc_ref, w_ref, *, n_slc, n_top):
    qt = pl.program_id(1) * NSA_TILES + sub
    R = N_HEADS_B * TQ
    t0 = qt * TQ
    rpg = N_HEADS_B // N_KV_B
    rows = rpg * TQ
    ncmp = kc_ref.shape[1]
    nwb = WINDOW_B // TQ
    s_prev = pl.multiple_of(jnp.maximum(t0 - TQ, 0), TQ)
    s_diag = pl.multiple_of(t0, TQ)
    lane = lax.broadcasted_iota(jnp.int32, (TQ, LANES), 1)
    rowi = lax.broadcasted_iota(jnp.int32, (TQ, LANES), 0)
    causal = lane <= rowi
    ones_col = (lane == 0).astype(BF16)
    has_prev = qt > 0

    c_idx = lax.broadcasted_iota(jnp.int32, (ncmp, LANES), 0)
    l_idx = lax.broadcasted_iota(jnp.int32, (ncmp, LANES), 1)
    place = ((c_idx - (TQ // CMP_STRIDE) * qt + CMP_NEAR_LO) == (l_idx & (CMP_NEAR - 1))) \
        & (l_idx < 2 * CMP_NEAR)
    kc_aug = jnp.concatenate([kc_ref[0], place.astype(BF16)], axis=1)
    qi = lax.broadcasted_iota(jnp.int32, (TQ, ncmp), 0)
    ci = lax.broadcasted_iota(jnp.int32, (TQ, ncmp), 1)
    mask_c = (t0 + qi - CMP_STRIDE * ci - (CMP_LEN - 1)) >= 0
    vc = vc_ref[0]

    cur = (t0 + rowi) // SEL_LEN
    valid = (lane <= cur) & (lane < n_slc)
    forced = (lane == 0) | (lane == cur) | (lane == cur - 1)
    cur0 = t0 // SEL_LEN
    near_lo = cur0 - TQ // SEL_LEN

    ks2 = jnp.concatenate(
        [jnp.concatenate([ks_ref[0, pl.ds(s_prev, TQ), :], oh_ref[pl.ds(s_prev, TQ), :]], axis=1),
         jnp.concatenate([ks_ref[0, pl.ds(s_diag, TQ), :], oh_ref[pl.ds(s_diag, TQ), :]], axis=1)],
        axis=0)
    vs2 = jnp.concatenate(
        [jnp.concatenate([vs_ref[0, pl.ds(s_prev, TQ), :], ones_col], axis=1),
         jnp.concatenate([vs_ref[0, pl.ds(s_diag, TQ), :], ones_col], axis=1)], axis=0)
    ones_far = jnp.concatenate([ones_col] * (FAR_TK // TQ), axis=0)

    w_starts = [pl.multiple_of(jnp.maximum(t0 - kb * TQ, 0), TQ) for kb in range(nwb, -1, -1)]
    kw5 = jnp.concatenate([kw_ref[0, pl.ds(st, TQ), :] for st in w_starts], axis=0)
    vw5 = jnp.concatenate([vw_ref[0, pl.ds(st, TQ), :] for st in w_starts], axis=0)

    nh = N_HEADS_B
    q_all = jnp.concatenate(
        [q_ref[0, sub * TQ:(sub + 1) * TQ, hh * LANES:(hh + 1) * LANES] for hh in range(nh)], axis=0)
    g2_all = jnp.concatenate([g2_ref[g] for g in range(N_KV_B)], axis=0)

    lc_all = _nt_dot(jnp.concatenate([q_all, g2_all], axis=1), kc_aug)
    w_ref[sub * R:(sub + 1) * R, :] = _nt_dot(q_all, kw5)
    pcs = [None] * N_KV_B
    pc_rows = []
    for hh in range(nh):
        lc = jnp.where(mask_c, lc_all[hh * TQ:(hh + 1) * TQ], NEG)
        e = jnp.where(mask_c, jnp.exp(lc - jnp.max(lc, axis=-1, keepdims=True)), 0.0)
        den = jnp.maximum(jnp.sum(e, axis=-1, keepdims=True), jnp.finfo(F32).tiny)
        pc = e / den
        pcs[hh // rpg] = pc if pcs[hh // rpg] is None else pcs[hh // rpg] + pc
        pc_rows.append(pc.astype(BF16))
    o_cmp_all = _dot(jnp.concatenate(pc_rows, axis=0), vc)

    pcs_all = jnp.concatenate(pcs, axis=0)
    pcs_hi = pcs_all.astype(BF16)
    pcs_lo = (pcs_all - pcs_hi.astype(F32)).astype(BF16)
    imp_all = _dot(pcs_hi, ovl_ref[...]) + _dot(pcs_lo, ovl_ref[...])
    s_ts = []
    for g in range(N_KV_B):
        score = jnp.where(valid, imp_all[g * TQ:(g + 1) * TQ] + jnp.where(forced, FORCE_BONUS, 0.0), NEG)
        s_ts.append(score.T[:n_slc])
    s_t = jnp.concatenate(s_ts, axis=1)
    n_io = lax.broadcasted_iota(jnp.int32, (n_slc, N_KV_B * TQ), 0)
    n_acc = 4
    ranks = [jnp.zeros((n_slc, N_KV_B * TQ), F32) for _ in range(n_acc)]
    for mm in range(n_slc):
        sm = s_t[mm:mm + 1, :]
        beats = (sm > s_t) | ((sm == s_t) & (n_io > mm))
        ranks[mm % n_acc] = ranks[mm % n_acc] + jnp.where(beats, 1.0, 0.0)
    rank = (ranks[0] + ranks[1]) + (ranks[2] + ranks[3])
    sel_t = jnp.where((rank < n_top) & (s_t > 0.5 * NEG), 1.0, 0.0)
    m_nears, m_fars = [], []
    for g in range(N_KV_B):
        st = sel_t[:, g * TQ:(g + 1) * TQ]
        if n_slc < LANES:
            st = jnp.concatenate([st, jnp.zeros((LANES - n_slc, TQ), F32)], axis=0)
        sel = st.T > 0.5
        m_nears.append(jnp.where(sel & (lane >= near_lo), 0.0, NEG).astype(BF16))
        m_fars.append(jnp.where(sel & (lane < near_lo), 0.0, NEG).astype(BF16))
    q_near = jnp.concatenate(
        [q_all, jnp.concatenate([m_nears[hh // rpg] for hh in range(nh)], axis=0)], axis=1)
    q_far = jnp.concatenate(
        [q_all, jnp.concatenate([m_fars[hh // rpg] for hh in range(nh)], axis=0)], axis=1)

    s_all = _nt_dot(q_near, ks2)
    s_rows = []
    for hh in range(nh):
        sr = s_all[hh * TQ:(hh + 1) * TQ]
        s_p = jnp.where(has_prev, sr[:, :TQ] + bprev_ref[hh], NEG)
        s_d = jnp.where(causal, sr[:, TQ:] + bdiag_ref[hh], NEG)
        s_rows.append(jnp.concatenate([s_p, s_d], axis=1))
    s_near = jnp.concatenate(s_rows, axis=0)
    m0 = jnp.max(s_near, axis=-1, keepdims=True)
    m_ref[sub * R:(sub + 1) * R, :] = jnp.broadcast_to(m0, (nh * TQ, LANES))
    acc_ref[sub * R:(sub + 1) * R, :] = _dot(jnp.exp(s_near - m0).astype(BF16), vs2)
    return q_far, (qt, o_cmp_all, vw5)


def _nsa_body(q_ref, kc_ref, vc_ref, ks_ref, vs_ref, kw_ref, vw_ref, oh_ref, gate_ref,
              bprev_ref, bdiag_ref, g2_ref, ovl_ref, o_ref, m_ref, acc_ref,
              s0_ref, s1_ref, mc0_ref, mc1_ref, w_ref, *, n_slc, n_top):
    q_fars, saved = [], []
    for sub in range(NSA_TILES):
        q_far_t, keep = _nsa_select(sub, q_ref, kc_ref, vc_ref, ks_ref, vs_ref, kw_ref, vw_ref, oh_ref,
                                    bprev_ref, bdiag_ref, g2_ref, ovl_ref, m_ref, acc_ref, w_ref,
                                    n_slc=n_slc, n_top=n_top)
        q_fars.append(q_far_t)
        saved.append(keep)
    q_far = jnp.concatenate(q_fars, axis=0)
    qt = saved[-1][0]
    lane = lax.broadcasted_iota(jnp.int32, (TQ, LANES), 1)
    ones_far = jnp.concatenate([(lane == 0).astype(BF16)] * (FAR_TK // TQ), axis=0)

    n_far = (qt + FAR_TK // TQ - 2) // (FAR_TK // TQ)
    last_chunk = ks_ref.shape[1] // FAR_TK - 1
    s_bufs = (s0_ref, s1_ref)
    mc_bufs = (mc0_ref, mc1_ref)

    def chunk_start(c):
        return pl.multiple_of(jnp.minimum(c, last_chunk) * FAR_TK, FAR_TK)

    def produce(slot, c):
        k0 = chunk_start(c)
        kk = jnp.concatenate([ks_ref[0, pl.ds(k0, FAR_TK), :], oh_ref[pl.ds(k0, FAR_TK), :]], axis=1)
        s = _nt_dot(q_far, kk)
        s_bufs[slot][...] = s
        mc_bufs[slot][...] = jnp.broadcast_to(jnp.max(s, axis=-1, keepdims=True), (s.shape[0], LANES))

    def consume(slot, c):
        k0 = chunk_start(c)
        vv = jnp.concatenate([vs_ref[0, pl.ds(k0, FAR_TK), :], ones_far], axis=1)
        m_old = m_ref[...]
        m_new = jnp.maximum(m_old, mc_bufs[slot][...])
        alpha = jnp.exp(m_old - m_new)
        p = jnp.exp(s_bufs[slot][...] - jnp.concatenate([m_new] * (FAR_TK // LANES), axis=1)).astype(BF16)
        acc_ref[...] = acc_ref[...] * jnp.concatenate([alpha] * 2, axis=1) + _dot(p, vv)
        m_ref[...] = m_new

    produce(0, 0)

    def far_pair(j, carry):
        produce(1, 2 * j + 1)
        consume(0, 2 * j)
        produce(0, 2 * j + 2)
        consume(1, 2 * j + 1)
        return carry

    lax.fori_loop(0, n_far // 2, far_pair, 0)

    @pl.when(n_far % 2 == 1)
    def _():
        consume(0, n_far - 1)

    acc = acc_ref[...]
    o_sel_all = acc[:, :LANES] / acc[:, LANES:LANES + 1]
    for sub in range(NSA_TILES):
        _nsa_finish(sub, saved[sub], o_sel_all, vw_ref, gate_ref, bprev_ref, bdiag_ref, w_ref, o_ref)


def _nsa_finish(sub, kept, o_sel_all, vw_ref, gate_ref, bprev_ref, bdiag_ref, w_ref, o_ref):
    qt, o_cmp_all, vw5 = kept
    nh = N_HEADS_B
    rpg = N_HEADS_B // N_KV_B
    nwb = WINDOW_B // TQ
    R = nh * TQ
    lane = lax.broadcasted_iota(jnp.int32, (TQ, LANES), 1)
    rowi = lax.broadcasted_iota(jnp.int32, (TQ, LANES), 0)
    causal = lane <= rowi
    has_prev = qt > 0

    pw, lw = [], []
    for hh in range(nh):
        wr = w_ref[sub * R + hh * TQ:sub * R + (hh + 1) * TQ, :]
        blocks = []
        for bi, kb in enumerate(range(nwb, -1, -1)):
            blk = wr[:, bi * TQ:(bi + 1) * TQ]
            if kb == nwb:
                blk = jnp.where((lane > rowi) & (qt >= kb), blk, NEG)
            elif kb >= 2:
                blk = jnp.where(qt >= kb, blk, NEG)
            elif kb == 1:
                blk = jnp.where(has_prev, blk + bprev_ref[hh], NEG)
            else:
                blk = jnp.where(causal, blk + bdiag_ref[hh], NEG)
            blocks.append(blk)
        sw = jnp.concatenate(blocks, axis=1)
        p = jnp.exp(sw - jnp.max(sw, axis=-1, keepdims=True))
        lw.append(jnp.sum(p, axis=-1, keepdims=True))
        pw.append(p.astype(BF16))
    o_win_all = _dot(jnp.concatenate(pw, axis=0), vw5)

    for g in range(N_KV_B):
        o_heads = []
        for r in range(rpg):
            hh = g * rpg + r
            sl = slice(hh * TQ, (hh + 1) * TQ)
            qrows = slice(sub * TQ, (sub + 1) * TQ)
            g_cmp = gate_ref[0, qrows, hh:hh + 1]
            g_sel = gate_ref[0, qrows, N_HEADS_B + hh:N_HEADS_B + hh + 1]
            g_win = gate_ref[0, qrows, 2 * N_HEADS_B + hh:2 * N_HEADS_B + hh + 1]
            o_sel = o_sel_all[sub * R + hh * TQ:sub * R + (hh + 1) * TQ]
            o_heads.append(g_cmp * o_cmp_all[sl] + g_sel * o_sel + g_win * (o_win_all[sl] / lw[hh]))
        for c, tile in enumerate(_merge_heads(o_heads, g, lane)):
            col = (g * rpg // 2 + c) * LANES
            o_ref[0, sub * TQ:(sub + 1) * TQ, col:col + LANES] = tile.astype(BF16)


def _nsa(qb, kc, vc, ks, vs, kw, vw, gates, bprev, bdiag, g2):
    b, s, _ = qb.shape
    assert s % FAR_TK == 0 and WINDOW_B % TQ == 0 and TQ % SEL_LEN == 0
    assert NSA_TILES == 2 and (FAR_TK // TQ) % NSA_TILES == 0
    n_slc = s // SEL_LEN
    assert n_slc <= LANES and n_slc % 8 == 0
    n_top = min(SEL_TOPK, n_slc)
    ncmp = s // CMP_STRIDE
    rpg = N_HEADS_B // N_KV_B
    kpos = np.arange(s)
    onehot = (kpos[:, None] // SEL_LEN == np.arange(LANES)[None, :]).astype(np.float32)
    cstart = np.arange(ncmp) * CMP_STRIDE
    sstart = np.arange(LANES) * SEL_LEN
    ovl = ((cstart[:, None] < sstart[None, :] + SEL_LEN)
           & (cstart[:, None] + CMP_LEN - 1 >= sstart[None, :])
           & (np.arange(ncmp)[:, None] < ncmp - CMP_LEN // CMP_STRIDE + 1)
           & (np.arange(LANES)[None, :] < n_slc)).astype(np.float32)
    full = lambda i, j: (i, 0, 0)
    return pl.pallas_call(
        functools.partial(_nsa_body, n_slc=n_slc, n_top=n_top),
        out_shape=jax.ShapeDtypeStruct((b, s, QB), BF16),
        grid=(b, s // (NSA_TILES * TQ)),
        in_specs=[pl.BlockSpec((1, NSA_TILES * TQ, N_HEADS_B * LANES), lambda i, j: (i, j, 0)),
                  pl.BlockSpec((1, ncmp, LANES), full), pl.BlockSpec((1, ncmp, LANES), full),
                  pl.BlockSpec((1, s, LANES), full), pl.BlockSpec((1, s, LANES), full),
                  pl.BlockSpec((1, s, LANES), full), pl.BlockSpec((1, s, LANES), full),
                  pl.BlockSpec((s, LANES), lambda i, j: (0, 0)),
                  pl.BlockSpec((1, NSA_TILES * TQ, LANES), lambda i, j: (i, j, 0)),
                  pl.BlockSpec((N_HEADS_B, TQ, TQ), lambda i, j: (0, 0, 0)),
                  pl.BlockSpec((N_HEADS_B, TQ, TQ), lambda i, j: (0, 0, 0)),
                  pl.BlockSpec((N_KV_B, rpg * TQ, LANES), lambda i, j: (0, 0, 0)),
                  pl.BlockSpec((ncmp, LANES), lambda i, j: (0, 0))],
        out_specs=pl.BlockSpec((1, NSA_TILES * TQ, QB), lambda i, j: (i, j, 0)),
        scratch_shapes=[pltpu.VMEM((NSA_TILES * N_KV_B * rpg * TQ, LANES), F32),
                        pltpu.VMEM((NSA_TILES * N_KV_B * rpg * TQ, 2 * LANES), F32),
                        pltpu.VMEM((NSA_TILES * N_KV_B * rpg * TQ, FAR_TK), F32),
                        pltpu.VMEM((NSA_TILES * N_KV_B * rpg * TQ, FAR_TK), F32),
                        pltpu.VMEM((NSA_TILES * N_KV_B * rpg * TQ, LANES), F32),
                        pltpu.VMEM((NSA_TILES * N_KV_B * rpg * TQ, LANES), F32),
                        pltpu.VMEM((NSA_TILES * N_KV_B * rpg * TQ, WINDOW_B + TQ), F32)],
        compiler_params=pltpu.CompilerParams(
            dimension_semantics=("arbitrary", "arbitrary"), vmem_limit_bytes=VMEM_LIMIT),
        name="nsa",
    )(qb, kc, vc, ks, vs, kw, vw, jnp.asarray(onehot, BF16), gates, bprev, bdiag, g2,
      jnp.asarray(ovl, BF16))


def _cmp_bias_operand(tbl_b):
    i = np.arange(TQ)[:, None]
    u = np.arange(CMP_NEAR)[None, :]
    rel = i - CMP_STRIDE * (u - CMP_NEAR_LO) - (CMP_LEN - 1)
    assert rel[0, 0] + CMP_STRIDE >= MAX_DISTANCE and rel[-1, -1] - CMP_STRIDE < 0
    near = (rel >= 0) & (rel < MAX_DISTANCE)
    far_const = tbl_b[NUM_BUCKETS - 1]
    vals = jnp.where(jnp.asarray(near)[None], _table_lookup(tbl_b, _bucket_np(rel))
                     - far_const[:, None, None], 0.0)
    hi = vals.astype(BF16)
    lo = (vals - hi.astype(F32)).astype(BF16)
    pad = jnp.zeros((N_HEADS_B, TQ, LANES - 2 * CMP_NEAR), BF16)
    out = jnp.concatenate([hi, lo, pad], axis=-1)
    rpg = N_HEADS_B // N_KV_B
    return out.reshape(N_KV_B, rpg * TQ, LANES)


def _layer(x, tbl, ffn1_norm, ffn1_w_in, ffn1_w_out, mix_norm, w_mix_in, w_mix_out,
           q_norm_a, k_norm_a, sinks_a, q_norm_b, k_norm_b,
           cmp_pos, cmp_w1, cmp_b1, cmp_w2, cmp_b2, ffn2_norm, ffn2_w_in, ffn2_w_out):
    b, s, d = x.shape
    x2d = x.reshape(b * s, d)
    x1 = _ffn(x2d, ffn1_norm, ffn1_w_in.astype(BF16), ffn1_w_out.astype(BF16))
    qa, qb, ka, ks, kw, va, vs, vw, kvc, gates = _proj(
        x1, mix_norm, w_mix_in, q_norm_a, k_norm_a, q_norm_b, k_norm_b)
    r3 = lambda a: a.reshape(b, s, a.shape[-1])
    kc, vc = _compress(r3(kvc), cmp_pos, cmp_w1, cmp_b1, cmp_w2, cmp_b2, k_norm_b)

    prev_t, diag_t = _band_tiles(tbl)
    bias_a = jnp.concatenate([prev_t[:N_HEADS_A], diag_t[:N_HEADS_A]], axis=-1)
    far_b = tbl[NUM_BUCKETS - 1, N_HEADS_A:].astype(F32)[:, None, None]
    bprev_b = prev_t[N_HEADS_A:] - far_b
    bdiag_b = diag_t[N_HEADS_A:] - far_b
    g2 = _cmp_bias_operand(tbl[:, N_HEADS_A:])

    mix_a = _swa(r3(qa), r3(ka), r3(va), bias_a, sinks_a)
    mix_b = _nsa(r3(qb), kc, vc, r3(ks), r3(vs), r3(kw), r3(vw), r3(gates), bprev_b, bdiag_b, g2)
    y = _ffn(x1, ffn2_norm, ffn2_w_in.astype(BF16), ffn2_w_out.astype(BF16),
             pre=(mix_a.reshape(b * s, QA), mix_b.reshape(b * s, QB), w_mix_out.astype(BF16)))
    return y.reshape(b, s, d)


def kernel(x, rel_bias_table, ffn1_norm, ffn1_w_in, ffn1_w_out, mix_norm, w_mix_in, w_mix_out,
           q_norm_a, k_norm_a, sinks_a, q_norm_b, k_norm_b,
           cmp_pos, cmp_w1, cmp_b1, cmp_w2, cmp_b2, ffn2_norm, ffn2_w_in, ffn2_w_out):
    for l in range(ffn1_norm.shape[0]):
        x = _layer(x, rel_bias_table, ffn1_norm[l], ffn1_w_in[l], ffn1_w_out[l],
                   mix_norm[l], w_mix_in[l], w_mix_out[l],
                   q_norm_a[l], k_norm_a[l], sinks_a[l], q_norm_b[l], k_norm_b[l],
                   cmp_pos[l], cmp_w1[l], cmp_b1[l], cmp_w2[l], cmp_b2[l],
                   ffn2_norm[l], ffn2_w_in[l], ffn2_w_out[l])
    return x
```

```python
import functools
import math

import numpy as np
import jax
import jax.numpy as jnp
from jax import lax
from jax.experimental import pallas as pl
from jax.experimental.pallas import tpu as pltpu

F32 = jnp.float32
BF16 = jnp.bfloat16

D_MODEL = 1024
HEAD_DIM = 64
N_HEADS_A = 8
N_KV_A = 2
N_HEADS_B = 8
N_KV_B = 2
N_HEADS = N_HEADS_A + N_HEADS_B
WINDOW_A = 128
CMP_LEN = 32
CMP_STRIDE = 16
CMP_HIDDEN = 128
SEL_LEN = 64
SEL_TOPK = 16
WINDOW_B = 512
FORCE_BONUS = 1000.0
NUM_BUCKETS = 32
MAX_DISTANCE = 128
D_FF = 2816
EPS = 1e-6
NEG = -1e9

QA = N_HEADS_A * HEAD_DIM
KVA = N_KV_A * HEAD_DIM
QB = N_HEADS_B * HEAD_DIM
KVB = N_KV_B * HEAD_DIM
N_GATES = 3 * N_HEADS_B
OFF_KA = QA
OFF_VA = OFF_KA + KVA
OFF_QB = OFF_VA + KVA
OFF_KVB = OFF_QB + QB
OFF_GB = OFF_KVB + 6 * KVB
D_IN = OFF_GB + N_GATES

LANES = 128
TQ = 128
FAR_TK = 512
NSA_TILES = 2
SWA_TILES = 4
FFN_TM = 512
FFN_TF = 256
VMEM_LIMIT = 56 * 1024 * 1024

P_QA = 0
P_QB = P_QA + QA
P_KA = P_QB + QB
P_KS = P_KA + LANES
P_KW = P_KS + LANES
P_VA = P_KW + LANES
P_VS = P_VA + LANES
P_VW = P_VS + LANES
P_KVC = P_VW + LANES
P_GATE = P_KVC + 2 * LANES
P_TOTAL = P_GATE + LANES
P_NORM = P_VA


def _nt_dot(a, b):
    return lax.dot_general(a, b, (((1,), (1,)), ((), ())), preferred_element_type=F32)


def _dot(a, b):
    return jnp.dot(a, b, preferred_element_type=F32)


def _rms_rows(x, g):
    ms = jnp.mean(x * x, axis=-1, keepdims=True)
    return x * lax.rsqrt(ms + EPS) * g


def _bucket_np(rel):
    n = np.maximum(rel, 0)
    max_exact = NUM_BUCKETS // 2
    nf = np.maximum(n, 1).astype(np.float64)
    large = max_exact + (np.log(nf / max_exact) / math.log(MAX_DISTANCE / max_exact)
                         * (NUM_BUCKETS - max_exact)).astype(np.int64)
    large = np.minimum(large, NUM_BUCKETS - 1)
    return np.where(n < max_exact, n, large).astype(np.int32)


def _ffn_body(*refs, pre):
    if pre:
        x_ref, ma_ref, mb_ref, wo_ref, nrm_ref, win_ref, wout_ref, o_ref, acc_ref = refs
    else:
        x_ref, nrm_ref, win_ref, wout_ref, o_ref, acc_ref = refs
    x = x_ref[...]
    if pre:
        x = x + _dot(ma_ref[...], wo_ref[0:QA, :]) + _dot(mb_ref[...], wo_ref[QA:QA + QB, :])
    h = _rms_rows(x, nrm_ref[...]).astype(BF16)
    for c in range(D_FF // FFN_TF):
        gate = _dot(h, win_ref[:, c * FFN_TF:(c + 1) * FFN_TF])
        up = _dot(h, win_ref[:, D_FF + c * FFN_TF:D_FF + (c + 1) * FFN_TF])
        a = (gate * jax.nn.sigmoid(gate) * up).astype(BF16)
        part = _dot(a, wout_ref[c * FFN_TF:(c + 1) * FFN_TF, :])
        if c == 0:
            acc_ref[...] = part
        else:
            acc_ref[...] += part
    o_ref[...] = x + 0.5 * acc_ref[...]


def _ffn(x2d, nrm, w_in, w_out, pre=None):
    m = x2d.shape[0]
    tm = FFN_TM
    assert m % tm == 0
    row = lambda i: (i, 0)
    const = lambda i: (0, 0)
    in_specs = [pl.BlockSpec((tm, D_MODEL), row)]
    args = [x2d]
    if pre is not None:
        mix_a, mix_b, wo = pre
        in_specs += [pl.BlockSpec((tm, QA), row), pl.BlockSpec((tm, QB), row),
                     pl.BlockSpec((QA + QB, D_MODEL), const)]
        args += [mix_a, mix_b, wo]
    in_specs += [pl.BlockSpec((1, D_MODEL), const),
                 pl.BlockSpec((D_MODEL, 2 * D_FF), const),
                 pl.BlockSpec((D_FF, D_MODEL), const)]
    args += [nrm.reshape(1, D_MODEL), w_in, w_out]
    return pl.pallas_call(
        functools.partial(_ffn_body, pre=pre is not None),
        out_shape=jax.ShapeDtypeStruct((m, D_MODEL), F32),
        grid=(m // tm,),
        in_specs=in_specs,
        out_specs=pl.BlockSpec((tm, D_MODEL), row),
        scratch_shapes=[pltpu.VMEM((tm, D_MODEL), F32)],
        compiler_params=pltpu.CompilerParams(
            dimension_semantics=("arbitrary",), vmem_limit_bytes=VMEM_LIMIT),
        name="ffn_pre" if pre is not None else "ffn",
    )(*args)


def _proj_body(x_ref, nrm_ref, w_ref, gain_ref, qa_ref, qb_ref, ka_ref, ks_ref, kw_ref,
               va_ref, vs_ref, vw_ref, kvc_ref, gate_ref):
    tm = x_ref.shape[0]
    h = _rms_rows(x_ref[...], nrm_ref[...]).astype(BF16)
    lo = lax.broadcasted_iota(jnp.int32, (tm, LANES), 1) < HEAD_DIM
    inv_d = 1.0 / HEAD_DIM

    def pair(c0):
        p2 = _dot(h, w_ref[:, c0:c0 + 2 * LANES])
        return p2[:, :LANES], p2[:, LANES:]

    def two_heads_f32(p, c0):
        sq = p * p
        ms0 = jnp.sum(jnp.where(lo, sq, 0.0), axis=-1, keepdims=True) * inv_d
        ms1 = jnp.sum(jnp.where(lo, 0.0, sq), axis=-1, keepdims=True) * inv_d
        r = jnp.where(lo, lax.rsqrt(ms0 + EPS), lax.rsqrt(ms1 + EPS))
        return p * r * gain_ref[:, c0:c0 + LANES]

    def two_heads(p, c0):
        return two_heads_f32(p, c0).astype(BF16)

    def spread_query_heads(p, c0, ref, t, n_kv):
        y = two_heads_f32(p, c0)
        swapped = pltpu.roll(y, HEAD_DIM, 1)
        g = (2 * t) // (ref.shape[1] // LANES // n_kv)
        even = jnp.where(lo, y, 0.0) if g == 0 else jnp.where(lo, 0.0, swapped)
        odd = jnp.where(lo, swapped, 0.0) if g == 0 else jnp.where(lo, 0.0, y)
        ref[:, 2 * t * LANES:(2 * t + 1) * LANES] = even.astype(BF16)
        ref[:, (2 * t + 1) * LANES:(2 * t + 2) * LANES] = odd.astype(BF16)

    for ref, base, n_kv in ((qa_ref, P_QA, N_KV_A), (qb_ref, P_QB, N_KV_B)):
        for pc in range(ref.shape[1] // LANES // 4):
            c0 = base + pc * 2 * LANES
            pa, pb = pair(c0)
            spread_query_heads(pa, c0, ref, 2 * pc, n_kv)
            spread_query_heads(pb, c0 + LANES, ref, 2 * pc + 1, n_kv)
    pa, pb = pair(P_KA)
    ka_ref[...] = two_heads(pa, P_KA)
    ks_ref[...] = two_heads(pb, P_KS)
    pa, pb = pair(P_KW)
    kw_ref[...] = two_heads(pa, P_KW)
    va_ref[...] = pb.astype(BF16)
    pa, pb = pair(P_VS)
    vs_ref[...] = pa.astype(BF16)
    vw_ref[...] = pb.astype(BF16)
    pa, pb = pair(P_KVC)
    kvc_ref[:, :LANES] = pa
    kvc_ref[:, LANES:] = pb
    pg = _dot(h, w_ref[:, P_GATE:P_GATE + LANES])
    gate_ref[...] = jax.nn.sigmoid(pg)


def _proj_src_columns():
    src = np.full((P_TOTAL,), -1, np.int64)
    gidx = np.full((P_NORM,), -1, np.int64)
    d = np.arange(HEAD_DIM)
    for hh in range(N_HEADS_A):
        src[P_QA + hh * HEAD_DIM + d] = hh * HEAD_DIM + d
        gidx[P_QA + hh * HEAD_DIM + d] = d
    for hh in range(N_HEADS_B):
        src[P_QB + hh * HEAD_DIM + d] = OFF_QB + hh * HEAD_DIM + d
        gidx[P_QB + hh * HEAD_DIM + d] = HEAD_DIM + d
    two = np.arange(2 * HEAD_DIM)
    src[P_KA + two] = OFF_KA + two
    gidx[P_KA + two] = 2 * HEAD_DIM + two % HEAD_DIM
    src[P_VA + two] = OFF_VA + two
    for slot, dst in ((0, P_KVC), (1, P_KVC + LANES), (2, P_KS), (3, P_VS), (4, P_KW), (5, P_VW)):
        src[dst + two] = OFF_KVB + slot * KVB + two
    gidx[P_KS + two] = 3 * HEAD_DIM + two % HEAD_DIM
    gidx[P_KW + two] = 3 * HEAD_DIM + two % HEAD_DIM
    src[P_GATE + np.arange(N_GATES)] = OFF_GB + np.arange(N_GATES)
    return src, gidx


def _proj(x2d, nrm, w_mix_in, q_norm_a, k_norm_a, q_norm_b, k_norm_b):
    m = x2d.shape[0]
    tm = FFN_TM
    src, gidx = _proj_src_columns()
    pieces, c = [], 0
    while c < P_TOTAL:
        e = c + 1
        while e < P_TOTAL and ((src[e] < 0) == (src[c] < 0)) and (src[c] < 0 or src[e] == src[e - 1] + 1):
            e += 1
        pieces.append(jnp.zeros((D_MODEL, e - c), w_mix_in.dtype) if src[c] < 0
                      else w_mix_in[:, src[c]:src[c] + e - c])
        c = e
    w = jnp.concatenate(pieces, axis=1).astype(BF16)
    scale = HEAD_DIM ** -0.5
    gsrc = jnp.concatenate([q_norm_a * scale, q_norm_b * scale, k_norm_a, k_norm_b]).astype(F32)
    gain = jnp.where(jnp.asarray(gidx >= 0), gsrc[np.maximum(gidx, 0)], 0.0).reshape(1, P_NORM)
    row = lambda i: (i, 0)
    const = lambda i: (0, 0)
    widths = [(N_HEADS_A * LANES, BF16), (N_HEADS_B * LANES, BF16)] + [(LANES, BF16)] * 6 + \
             [(2 * LANES, F32), (LANES, F32)]
    return pl.pallas_call(
        _proj_body,
        out_shape=[jax.ShapeDtypeStruct((m, wd), dt) for wd, dt in widths],
        grid=(m // tm,),
        in_specs=[pl.BlockSpec((tm, D_MODEL), row), pl.BlockSpec((1, D_MODEL), const),
                  pl.BlockSpec((D_MODEL, P_TOTAL), const), pl.BlockSpec((1, P_NORM), const)],
        out_specs=[pl.BlockSpec((tm, wd), row) for wd, _ in widths],
        compiler_params=pltpu.CompilerParams(
            dimension_semantics=("arbitrary",), vmem_limit_bytes=VMEM_LIMIT),
        name="proj",
    )(x2d, nrm.reshape(1, D_MODEL), w, gain)


CMP_FEAT = CMP_STRIDE * 2 * LANES
CMP_SLOTS = 4


def _compress_body(x_ref, pos_ref, w1_ref, b1_ref, w2_ref, b2_ref, gain_ref, kc_ref, vc_ref):
    nck = x_ref.shape[1]
    x = x_ref[0]
    a = _dot((x + pos_ref[0]).astype(BF16), w1_ref[0])
    b = _dot((x + pos_ref[1]).astype(BF16), w1_ref[1])
    pre = a + pltpu.roll(b, nck - 1, 0) + b1_ref[...]
    hid = (pre * jax.nn.sigmoid(pre)).astype(BF16)
    out = _dot(hid, w2_ref[...]) + b2_ref[...]
    k = out[:, :LANES]
    lo = lax.broadcasted_iota(jnp.int32, (nck, LANES), 1) < HEAD_DIM
    sq = k * k
    ms0 = jnp.sum(jnp.where(lo, sq, 0.0), axis=-1, keepdims=True) * (1.0 / HEAD_DIM)
    ms1 = jnp.sum(jnp.where(lo, 0.0, sq), axis=-1, keepdims=True) * (1.0 / HEAD_DIM)
    r = jnp.where(lo, lax.rsqrt(ms0 + EPS), lax.rsqrt(ms1 + EPS))
    kc_ref[0] = (k * r * gain_ref[...]).astype(BF16)
    vc_ref[0] = out[:, LANES:].astype(BF16)


def _compress(kvc, cmp_pos, cmp_w1, cmp_b1, cmp_w2, cmp_b2, k_norm_b):
    b, s, _ = kvc.shape
    nck = s // CMP_STRIDE
    x = kvc.reshape(b, nck, CMP_FEAT)
    r = CMP_LEN // CMP_STRIDE
    w1r = cmp_w1.reshape(2, r, CMP_STRIDE, HEAD_DIM, CMP_HIDDEN)
    eye = jnp.eye(CMP_SLOTS, dtype=F32)
    kv_of_slot = np.array([0, 0, 1, 1])
    w1s = w1r[kv_of_slot]
    w1big = jnp.einsum('sjpdh,st->jpsdth', w1s, eye).reshape(r, CMP_FEAT, CMP_SLOTS * CMP_HIDDEN)
    posr = cmp_pos.reshape(2, r, CMP_STRIDE, HEAD_DIM)[kv_of_slot]
    posbig = posr.transpose(1, 2, 0, 3).reshape(r, 1, CMP_FEAT)
    b1big = cmp_b1[kv_of_slot].reshape(1, CMP_SLOTS * CMP_HIDDEN)
    w2big = jnp.einsum('shd,st->shtd', cmp_w2[kv_of_slot], eye).reshape(
        CMP_SLOTS * CMP_HIDDEN, CMP_SLOTS * HEAD_DIM)
    b2big = cmp_b2[kv_of_slot].reshape(1, CMP_SLOTS * HEAD_DIM)
    gain = jnp.tile(k_norm_b, 2).reshape(1, LANES)
    assert r == 2
    c3 = lambda i: (0, 0, 0)
    c2 = lambda i: (0, 0)
    return pl.pallas_call(
        _compress_body,
        out_shape=[jax.ShapeDtypeStruct((b, nck, LANES), BF16)] * 2,
        grid=(b,),
        in_specs=[pl.BlockSpec((1, nck, CMP_FEAT), lambda i: (i, 0, 0)),
                  pl.BlockSpec((r, 1, CMP_FEAT), c3),
                  pl.BlockSpec((r, CMP_FEAT, CMP_SLOTS * CMP_HIDDEN), c3),
                  pl.BlockSpec((1, CMP_SLOTS * CMP_HIDDEN), c2),
                  pl.BlockSpec((CMP_SLOTS * CMP_HIDDEN, CMP_SLOTS * HEAD_DIM), c2),
                  pl.BlockSpec((1, CMP_SLOTS * HEAD_DIM), c2),
                  pl.BlockSpec((1, LANES), c2)],
        out_specs=[pl.BlockSpec((1, nck, LANES), lambda i: (i, 0, 0))] * 2,
        compiler_params=pltpu.CompilerParams(
            dimension_semantics=("arbitrary",), vmem_limit_bytes=VMEM_LIMIT),
        name="compress",
    )(x, posbig, w1big.astype(BF16), b1big, w2big.astype(BF16), b2big, gain)


def _merge_heads(o_heads, g, lane):
    tiles = []
    for c in range(len(o_heads) // 2):
        lo_src = o_heads[2 * c] if g == 0 else pltpu.roll(o_heads[2 * c], HEAD_DIM, 1)
        hi_src = o_heads[2 * c + 1] if g == 1 else pltpu.roll(o_heads[2 * c + 1], HEAD_DIM, 1)
        tiles.append(jnp.where(lane < HEAD_DIM, lo_src, hi_src))
    return tiles


def _band_tiles(tbl):
    i = np.arange(TQ)[:, None]
    j = np.arange(TQ)[None, :]
    return _table_lookup(tbl, _bucket_np(i - j + TQ)), _table_lookup(tbl, _bucket_np(i - j))


def _table_lookup(tbl, buckets):
    onehot = (buckets.reshape(-1)[:, None] == np.arange(NUM_BUCKETS)[None, :]).astype(np.float32)
    out = jnp.einsum('pk,kh->hp', jnp.asarray(onehot), tbl.astype(F32), precision=lax.Precision.HIGHEST)
    return out.reshape((tbl.shape[1],) + buckets.shape)


def _swa_body(sink_ref, q_ref, k_ref, v_ref, bias_ref, o_ref):
    ii = lax.broadcasted_iota(jnp.int32, (TQ, 2 * TQ), 0)
    jj = lax.broadcasted_iota(jnp.int32, (TQ, 2 * TQ), 1)
    lane = lax.broadcasted_iota(jnp.int32, (TQ, LANES), 1)
    rpg = N_HEADS_A // N_KV_A
    for sub in range(SWA_TILES):
        qt = pl.program_id(1) * SWA_TILES + sub
        t0 = qt * TQ
        rows = slice(sub * TQ, (sub + 1) * TQ)
        s_prev = pl.multiple_of(jnp.maximum(t0 - TQ, 0), TQ)
        s_diag = pl.multiple_of(t0, TQ)
        k2 = jnp.concatenate([k_ref[0, pl.ds(s_prev, TQ), :], k_ref[0, pl.ds(s_diag, TQ), :]], axis=0)
        v2 = jnp.concatenate([v_ref[0, pl.ds(s_prev, TQ), :], v_ref[0, pl.ds(s_diag, TQ), :]], axis=0)
        mask = ((jj < TQ) & (jj > ii) & (qt > 0)) | ((jj >= TQ) & ((jj - TQ) <= ii))
        q_all = jnp.concatenate(
            [q_ref[0, rows, hh * LANES:(hh + 1) * LANES] for hh in range(N_HEADS_A)], axis=0)
        s_all = _nt_dot(q_all, k2)
        ps, dens = [], []
        for hh in range(N_HEADS_A):
            s = jnp.where(mask, s_all[hh * TQ:(hh + 1) * TQ] + bias_ref[hh], NEG)
            sink = sink_ref[hh]
            m = jnp.maximum(jnp.max(s, axis=-1, keepdims=True), sink)
            p = jnp.exp(s - m)
            dens.append(jnp.sum(p, axis=-1, keepdims=True) + jnp.exp(sink - m))
            ps.append(p.astype(BF16))
        o_all = _dot(jnp.concatenate(ps, axis=0), v2)
        for g in range(N_KV_A):
            o_heads = [o_all[(g * rpg + r) * TQ:(g * rpg + r + 1) * TQ] / dens[g * rpg + r]
                       for r in range(rpg)]
            for c, tile in enumerate(_merge_heads(o_heads, g, lane)):
                col = (g * rpg // 2 + c) * LANES
                o_ref[0, rows, col:col + LANES] = tile.astype(BF16)


def _swa(qa, ka, va, bias, sinks):
    b, s, _ = qa.shape
    assert WINDOW_A == TQ
    return pl.pallas_call(
        _swa_body,
        out_shape=jax.ShapeDtypeStruct((b, s, QA), BF16),
        grid=(b, s // (SWA_TILES * TQ)),
        in_specs=[pl.BlockSpec(memory_space=pltpu.SMEM),
                  pl.BlockSpec((1, SWA_TILES * TQ, N_HEADS_A * LANES), lambda i, j: (i, j, 0)),
                  pl.BlockSpec((1, s, LANES), lambda i, j: (i, 0, 0)),
                  pl.BlockSpec((1, s, LANES), lambda i, j: (i, 0, 0)),
                  pl.BlockSpec((N_HEADS_A, TQ, 2 * TQ), lambda i, j: (0, 0, 0))],
        out_specs=pl.BlockSpec((1, SWA_TILES * TQ, QA), lambda i, j: (i, j, 0)),
        compiler_params=pltpu.CompilerParams(
            dimension_semantics=("arbitrary", "arbitrary"), vmem_limit_bytes=VMEM_LIMIT),
        name="swa",
    )(sinks.astype(F32), qa, ka, va, bias)


CMP_NEAR = 16
CMP_NEAR_LO = 9


def _nsa_select(sub, q_ref, kc_ref, vc_ref, ks_ref, vs_ref, kw_ref, vw_ref, oh_ref,
                bprev_ref, bdiag_ref, g2_ref, ovl_ref, m_ref, acc_ref, w_ref, *, n_slc, n_top):
    qt = pl.program_id(1) * NSA_TILES + sub
    R = N_HEADS_B * TQ
    t0 = qt * TQ
    rpg = N_HEADS_B // N_KV_B
    ncmp = kc_ref.shape[1]
    nwb = WINDOW_B // TQ
    s_prev = pl.multiple_of(jnp.maximum(t0 - TQ, 0), TQ)
    s_diag = pl.multiple_of(t0, TQ)
    lane = lax.broadcasted_iota(jnp.int32, (TQ, LANES), 1)
    rowi = lax.broadcasted_iota(jnp.int32, (TQ, LANES), 0)
    causal = lane <= rowi
    ones_col = (lane == 0).astype(BF16)
    has_prev = qt > 0

    c_idx = lax.broadcasted_iota(jnp.int32, (ncmp, LANES), 0)
    l_idx = lax.broadcasted_iota(jnp.int32, (ncmp, LANES), 1)
    place = ((c_idx - (TQ // CMP_STRIDE) * qt + CMP_NEAR_LO) == (l_idx & (CMP_NEAR - 1))) \
        & (l_idx < 2 * CMP_NEAR)
    kc_aug = jnp.concatenate([kc_ref[0], place.astype(BF16)], axis=1)
    qi = lax.broadcasted_iota(jnp.int32, (TQ, ncmp), 0)
    ci = lax.broadcasted_iota(jnp.int32, (TQ, ncmp), 1)
    mask_c = (t0 + qi - CMP_STRIDE * ci - (CMP_LEN - 1)) >= 0
    vc = vc_ref[0]

    cur = (t0 + rowi) // SEL_LEN
    valid = (lane <= cur) & (lane < n_slc)
    forced = (lane == 0) | (lane == cur) | (lane == cur - 1)
    cur0 = t0 // SEL_LEN
    near_lo = cur0 - TQ // SEL_LEN

    ks2 = jnp.concatenate(
        [jnp.concatenate([ks_ref[0, pl.ds(s_prev, TQ), :], oh_ref[pl.ds(s_prev, TQ), :]], axis=1),
         jnp.concatenate([ks_ref[0, pl.ds(s_diag, TQ), :], oh_ref[pl.ds(s_diag, TQ), :]], axis=1)],
        axis=0)
    vs2 = jnp.concatenate(
        [jnp.concatenate([vs_ref[0, pl.ds(s_prev, TQ), :], ones_col], axis=1),
         jnp.concatenate([vs_ref[0, pl.ds(s_diag, TQ), :], ones_col], axis=1)], axis=0)
    ones_far = jnp.concatenate([ones_col] * (FAR_TK // TQ), axis=0)

    w_starts = [pl.multiple_of(jnp.maximum(t0 - kb * TQ, 0), TQ) for kb in range(nwb, -1, -1)]
    kw5 = jnp.concatenate([kw_ref[0, pl.ds(st, TQ), :] for st in w_starts], axis=0)
    vw5 = jnp.concatenate([vw_ref[0, pl.ds(st, TQ), :] for st in w_starts], axis=0)

    nh = N_HEADS_B
    q_all = jnp.concatenate(
        [q_ref[0, sub * TQ:(sub + 1) * TQ, hh * LANES:(hh + 1) * LANES] for hh in range(nh)], axis=0)
    g2_all = jnp.concatenate([g2_ref[g] for g in range(N_KV_B)], axis=0)

    lc_all = _nt_dot(jnp.concatenate([q_all, g2_all], axis=1), kc_aug)
    w_ref[sub * R:(sub + 1) * R, :] = _nt_dot(q_all, kw5)
    pcs = [None] * N_KV_B
    pc_rows = []
    for hh in range(nh):
        lc = jnp.where(mask_c, lc_all[hh * TQ:(hh + 1) * TQ], NEG)
        e = jnp.where(mask_c, jnp.exp(lc - jnp.max(lc, axis=-1, keepdims=True)), 0.0)
        den = jnp.maximum(jnp.sum(e, axis=-1, keepdims=True), jnp.finfo(F32).tiny)
        pc = e / den
        pcs[hh // rpg] = pc if pcs[hh // rpg] is None else pcs[hh // rpg] + pc
        pc_rows.append(pc.astype(BF16))
    o_cmp_all = _dot(jnp.concatenate(pc_rows, axis=0), vc)

    pcs_all = jnp.concatenate(pcs, axis=0)
    pcs_hi = pcs_all.astype(BF16)
    pcs_lo = (pcs_all - pcs_hi.astype(F32)).astype(BF16)
    imp_all = _dot(pcs_hi, ovl_ref[...]) + _dot(pcs_lo, ovl_ref[...])
    s_ts = []
    for g in range(N_KV_B):
        score = jnp.where(valid, imp_all[g * TQ:(g + 1) * TQ] + jnp.where(forced, FORCE_BONUS, 0.0), NEG)
        s_ts.append(score.T[:n_slc])
    s_t = jnp.concatenate(s_ts, axis=1)
    n_io = lax.broadcasted_iota(jnp.int32, (n_slc, N_KV_B * TQ), 0)
    n_acc = 4
    ranks = [jnp.zeros((n_slc, N_KV_B * TQ), F32) for _ in range(n_acc)]
    for mm in range(n_slc):
        sm = s_t[mm:mm + 1, :]
        beats = (sm > s_t) | ((sm == s_t) & (n_io > mm))
        ranks[mm % n_acc] = ranks[mm % n_acc] + jnp.where(beats, 1.0, 0.0)
    rank = (ranks[0] + ranks[1]) + (ranks[2] + ranks[3])
    sel_t = jnp.where((rank < n_top) & (s_t > 0.5 * NEG), 1.0, 0.0)
    m_nears, m_fars = [], []
    for g in range(N_KV_B):
        st = sel_t[:, g * TQ:(g + 1) * TQ]
        if n_slc < LANES:
            st = jnp.concatenate([st, jnp.zeros((LANES - n_slc, TQ), F32)], axis=0)
        sel = st.T > 0.5
        m_nears.append(jnp.where(sel & (lane >= near_lo), 0.0, NEG).astype(BF16))
        m_fars.append(jnp.where(sel & (lane < near_lo), 0.0, NEG).astype(BF16))
    q_near = jnp.concatenate(
        [q_all, jnp.concatenate([m_nears[hh // rpg] for hh in range(nh)], axis=0)], axis=1)
    q_far = jnp.concatenate(
        [q_all, jnp.concatenate([m_fars[hh // rpg] for hh in range(nh)], axis=0)], axis=1)

    s_all = _nt_dot(q_near, ks2)
    s_rows = []
    for hh in range(nh):
        sr = s_all[hh * TQ:(hh + 1) * TQ]
        s_p = jnp.where(has_prev, sr[:, :TQ] + bprev_ref[hh], NEG)
        s_d = jnp.where(causal, sr[:, TQ:] + bdiag_ref[hh], NEG)
        s_rows.append(jnp.concatenate([s_p, s_d], axis=1))
    s_near = jnp.concatenate(s_rows, axis=0)
    m0 = jnp.max(s_near, axis=-1, keepdims=True)
    m_ref[sub * R:(sub + 1) * R, :] = jnp.broadcast_to(m0, (nh * TQ, LANES))
    acc_ref[sub * R:(sub + 1) * R, :] = _dot(jnp.exp(s_near - m0).astype(BF16), vs2)
    return q_far, (qt, o_cmp_all, vw5)


def _nsa_body(q_ref, kc_ref, vc_ref, ks_ref, vs_ref, kw_ref, vw_ref, oh_ref, gate_ref,
              bprev_ref, bdiag_ref, g2_ref, ovl_ref, o_ref, m_ref, acc_ref,
              s0_ref, s1_ref, mc0_ref, mc1_ref, w_ref, *, n_slc, n_top):
    q_fars, saved = [], []
    for sub in range(NSA_TILES):
        q_far_t, keep = _nsa_select(sub, q_ref, kc_ref, vc_ref, ks_ref, vs_ref, kw_ref, vw_ref, oh_ref,
                                    bprev_ref, bdiag_ref, g2_ref, ovl_ref, m_ref, acc_ref, w_ref,
                                    n_slc=n_slc, n_top=n_top)
        q_fars.append(q_far_t)
        saved.append(keep)
    q_far = jnp.concatenate(q_fars, axis=0)
    qt = saved[-1][0]
    lane = lax.broadcasted_iota(jnp.int32, (TQ, LANES), 1)
    ones_far = jnp.concatenate([(lane == 0).astype(BF16)] * (FAR_TK // TQ), axis=0)

    n_far = (qt + FAR_TK // TQ - 2) // (FAR_TK // TQ)
    last_chunk = ks_ref.shape[1] // FAR_TK - 1
    s_bufs = (s0_ref, s1_ref)
    mc_bufs = (mc0_ref, mc1_ref)

    def chunk_start(c):
        return pl.multiple_of(jnp.minimum(c, last_chunk) * FAR_TK, FAR_TK)

    def produce(slot, c):
        k0 = chunk_start(c)
        kk = jnp.concatenate([ks_ref[0, pl.ds(k0, FAR_TK), :], oh_ref[pl.ds(k0, FAR_TK), :]], axis=1)
        s = _nt_dot(q_far, kk)
        s_bufs[slot][...] = s
        mc_bufs[slot][...] = jnp.broadcast_to(jnp.max(s, axis=-1, keepdims=True), (s.shape[0], LANES))

    def consume(slot, c):
        k0 = chunk_start(c)
        vv = jnp.concatenate([vs_ref[0, pl.ds(k0, FAR_TK), :], ones_far], axis=1)
        m_old = m_ref[...]
        m_new = jnp.maximum(m_old, mc_bufs[slot][...])
        alpha = jnp.exp(m_old - m_new)
        p = jnp.exp(s_bufs[slot][...] - jnp.concatenate([m_new] * (FAR_TK // LANES), axis=1)).astype(BF16)
        acc_ref[...] = acc_ref[...] * jnp.concatenate([alpha] * 2, axis=1) + _dot(p, vv)
        m_ref[...] = m_new

    produce(0, 0)

    def far_pair(j, carry):
        produce(1, 2 * j + 1)
        consume(0, 2 * j)
        produce(0, 2 * j + 2)
        consume(1, 2 * j + 1)
        return carry

    lax.fori_loop(0, n_far // 2, far_pair, 0)

    @pl.when(n_far % 2 == 1)
    def _():
        consume(0, n_far - 1)

    acc = acc_ref[...]
    o_sel_all = acc[:, :LANES] / acc[:, LANES:LANES + 1]
    for sub in range(NSA_TILES):
        _nsa_finish(sub, saved[sub], o_sel_all, vw_ref, gate_ref, bprev_ref, bdiag_ref, w_ref, o_ref)


def _nsa_finish(sub, kept, o_sel_all, vw_ref, gate_ref, bprev_ref, bdiag_ref, w_ref, o_ref):
    qt, o_cmp_all, vw5 = kept
    nh = N_HEADS_B
    rpg = N_HEADS_B // N_KV_B
    nwb = WINDOW_B // TQ
    R = nh * TQ
    lane = lax.broadcasted_iota(jnp.int32, (TQ, LANES), 1)
    rowi = lax.broadcasted_iota(jnp.int32, (TQ, LANES), 0)
    causal = lane <= rowi
    has_prev = qt > 0

    pw, lw = [], []
    for hh in range(nh):
        wr = w_ref[sub * R + hh * TQ:sub * R + (hh + 1) * TQ, :]
        blocks = []
        for bi, kb in enumerate(range(nwb, -1, -1)):
            blk = wr[:, bi * TQ:(bi + 1) * TQ]
            if kb == nwb:
                blk = jnp.where((lane > rowi) & (qt >= kb), blk, NEG)
            elif kb >= 2:
                blk = jnp.where(qt >= kb, blk, NEG)
            elif kb == 1:
                blk = jnp.where(has_prev, blk + bprev_ref[hh], NEG)
            else:
                blk = jnp.where(causal, blk + bdiag_ref[hh], NEG)
            blocks.append(blk)
        sw = jnp.concatenate(blocks, axis=1)
        p = jnp.exp(sw - jnp.max(sw, axis=-1, keepdims=True))
        lw.append(jnp.sum(p, axis=-1, keepdims=True))
        pw.append(p.astype(BF16))
    o_win_all = _dot(jnp.concatenate(pw, axis=0), vw5)

    for g in range(N_KV_B):
        o_heads = []
        for r in range(rpg):
            hh = g * rpg + r
            sl = slice(hh * TQ, (hh + 1) * TQ)
            qrows = slice(sub * TQ, (sub + 1) * TQ)
            g_cmp = gate_ref[0, qrows, hh:hh + 1]
            g_sel = gate_ref[0, qrows, N_HEADS_B + hh:N_HEADS_B + hh + 1]
            g_win = gate_ref[0, qrows, 2 * N_HEADS_B + hh:2 * N_HEADS_B + hh + 1]
            o_sel = o_sel_all[sub * R + hh * TQ:sub * R + (hh + 1) * TQ]
            o_heads.append(g_cmp * o_cmp_all[sl] + g_sel * o_sel + g_win * (o_win_all[sl] / lw[hh]))
        for c, tile in enumerate(_merge_heads(o_heads, g, lane)):
            col = (g * rpg // 2 + c) * LANES
            o_ref[0, sub * TQ:(sub + 1) * TQ, col:col + LANES] = tile.astype(BF16)


def _nsa(qb, kc, vc, ks, vs, kw, vw, gates, bprev, bdiag, g2):
    b, s, _ = qb.shape
    assert s % FAR_TK == 0 and WINDOW_B % TQ == 0 and TQ % SEL_LEN == 0
    assert NSA_TILES == 2 and (FAR_TK // TQ) % NSA_TILES == 0
    n_slc = s // SEL_LEN
    assert n_slc <= LANES and n_slc % 8 == 0
    n_top = min(SEL_TOPK, n_slc)
    ncmp = s // CMP_STRIDE
    rpg = N_HEADS_B // N_KV_B
    kpos = np.arange(s)
    onehot = (kpos[:, None] // SEL_LEN == np.arange(LANES)[None, :]).astype(np.float32)
    cstart = np.arange(ncmp) * CMP_STRIDE
    sstart = np.arange(LANES) * SEL_LEN
    ovl = ((cstart[:, None] < sstart[None, :] + SEL_LEN)
           & (cstart[:, None] + CMP_LEN - 1 >= sstart[None, :])
           & (np.arange(ncmp)[:, None] < ncmp - CMP_LEN // CMP_STRIDE + 1)
           & (np.arange(LANES)[None, :] < n_slc)).astype(np.float32)
    full = lambda i, j: (i, 0, 0)
    return pl.pallas_call(
        functools.partial(_nsa_body, n_slc=n_slc, n_top=n_top),
        out_shape=jax.ShapeDtypeStruct((b, s, QB), BF16),
        grid=(b, s // (NSA_TILES * TQ)),
        in_specs=[pl.BlockSpec((1, NSA_TILES * TQ, N_HEADS_B * LANES), lambda i, j: (i, j, 0)),
                  pl.BlockSpec((1, ncmp, LANES), full), pl.BlockSpec((1, ncmp, LANES), full),
                  pl.BlockSpec((1, s, LANES), full), pl.BlockSpec((1, s, LANES), full),
                  pl.BlockSpec((1, s, LANES), full), pl.BlockSpec((1, s, LANES), full),
                  pl.BlockSpec((s, LANES), lambda i, j: (0, 0)),
                  pl.BlockSpec((1, NSA_TILES * TQ, LANES), lambda i, j: (i, j, 0)),
                  pl.BlockSpec((N_HEADS_B, TQ, TQ), lambda i, j: (0, 0, 0)),
                  pl.BlockSpec((N_HEADS_B, TQ, TQ), lambda i, j: (0, 0, 0)),
                  pl.BlockSpec((N_KV_B, rpg * TQ, LANES), lambda i, j: (0, 0, 0)),
                  pl.BlockSpec((ncmp, LANES), lambda i, j: (0, 0))],
        out_specs=pl.BlockSpec((1, NSA_TILES * TQ, QB), lambda i, j: (i, j, 0)),
        scratch_shapes=[pltpu.VMEM((NSA_TILES * N_KV_B * rpg * TQ, LANES), F32),
                        pltpu.VMEM((NSA_TILES * N_KV_B * rpg * TQ, 2 * LANES), F32),
                        pltpu.VMEM((NSA_TILES * N_KV_B * rpg * TQ, FAR_TK), F32),
                        pltpu.VMEM((NSA_TILES * N_KV_B * rpg * TQ, FAR_TK), F32),
                        pltpu.VMEM((NSA_TILES * N_KV_B * rpg * TQ, LANES), F32),
                        pltpu.VMEM((NSA_TILES * N_KV_B * rpg * TQ, LANES), F32),
                        pltpu.VMEM((NSA_TILES * N_KV_B * rpg * TQ, WINDOW_B + TQ), F32)],
        compiler_params=pltpu.CompilerParams(
            dimension_semantics=("arbitrary", "arbitrary"), vmem_limit_bytes=VMEM_LIMIT),
        name="nsa",
    )(qb, kc, vc, ks, vs, kw, vw, jnp.asarray(onehot, BF16), gates, bprev, bdiag, g2,
      jnp.asarray(ovl, BF16))


def _cmp_bias_operand(tbl_b):
    i = np.arange(TQ)[:, None]
    u = np.arange(CMP_NEAR)[None, :]
    rel = i - CMP_STRIDE * (u - CMP_NEAR_LO) - (CMP_LEN - 1)
    assert rel[0, 0] + CMP_STRIDE >= MAX_DISTANCE and rel[-1, -1] - CMP_STRIDE < 0
    near = (rel >= 0) & (rel < MAX_DISTANCE)
    far_const = tbl_b[NUM_BUCKETS - 1]
    vals = jnp.where(jnp.asarray(near)[None], _table_lookup(tbl_b, _bucket_np(rel))
                     - far_const[:, None, None], 0.0)
    hi = vals.astype(BF16)
    lo = (vals - hi.astype(F32)).astype(BF16)
    pad = jnp.zeros((N_HEADS_B, TQ, LANES - 2 * CMP_NEAR), BF16)
    out = jnp.concatenate([hi, lo, pad], axis=-1)
    rpg = N_HEADS_B // N_KV_B
    return out.reshape(N_KV_B, rpg * TQ, LANES)


def _layer(x, tbl, ffn1_norm, ffn1_w_in, ffn1_w_out, mix_norm, w_mix_in, w_mix_out,
           q_norm_a, k_norm_a, sinks_a, q_norm_b, k_norm_b,
           cmp_pos, cmp_w1, cmp_b1, cmp_w2, cmp_b2, ffn2_norm, ffn2_w_in, ffn2_w_out):
    b, s, d = x.shape
    x2d = x.reshape(b * s, d)
    x1 = _ffn(x2d, ffn1_norm, ffn1_w_in.astype(BF16), ffn1_w_out.astype(BF16))
    qa, qb, ka, ks, kw, va, vs, vw, kvc, gates = _proj(
        x1, mix_norm, w_mix_in, q_norm_a, k_norm_a, q_norm_b, k_norm_b)
    r3 = lambda a: a.reshape(b, s, a.shape[-1])
    kc, vc = _compress(r3(kvc), cmp_pos, cmp_w1, cmp_b1, cmp_w2, cmp_b2, k_norm_b)

    prev_t, diag_t = _band_tiles(tbl)
    bias_a = jnp.concatenate([prev_t[:N_HEADS_A], diag_t[:N_HEADS_A]], axis=-1)
    far_b = tbl[NUM_BUCKETS - 1, N_HEADS_A:].astype(F32)[:, None, None]
    bprev_b = prev_t[N_HEADS_A:] - far_b
    bdiag_b = diag_t[N_HEADS_A:] - far_b
    g2 = _cmp_bias_operand(tbl[:, N_HEADS_A:])

    mix_a = _swa(r3(qa), r3(ka), r3(va), bias_a, sinks_a)
    mix_b = _nsa(r3(qb), kc, vc, r3(ks), r3(vs), r3(kw), r3(vw), r3(gates), bprev_b, bdiag_b, g2)
    y = _ffn(x1, ffn2_norm, ffn2_w_in.astype(BF16), ffn2_w_out.astype(BF16),
             pre=(mix_a.reshape(b * s, QA), mix_b.reshape(b * s, QB), w_mix_out.astype(BF16)))
    return y.reshape(b, s, d)


def kernel(x, rel_bias_table, ffn1_norm, ffn1_w_in, ffn1_w_out, mix_norm, w_mix_in, w_mix_out,
           q_norm_a, k_norm_a, sinks_a, q_norm_b, k_norm_b,
           cmp_pos, cmp_w1, cmp_b1, cmp_w2, cmp_b2, ffn2_norm, ffn2_w_in, ffn2_w_out):
    for l in range(ffn1_norm.shape[0]):
        x = _layer(x, rel_bias_table, ffn1_norm[l], ffn1_w_in[l], ffn1_w_out[l],
                   mix_norm[l], w_mix_in[l], w_mix_out[l],
                   q_norm_a[l], k_norm_a[l], sinks_a[l], q_norm_b[l], k_norm_b[l],
                   cmp_pos[l], cmp_w1[l], cmp_b1[l], cmp_w2[l], cmp_b2[l],
                   ffn2_norm[l], ffn2_w_in[l], ffn2_w_out[l])
    return x
```

```python
import functools
import math

import numpy as np
import jax
import jax.numpy as jnp
from jax import lax
from jax.experimental import pallas as pl
from jax.experimental.pallas import tpu as pltpu

F32 = jnp.float32
BF16 = jnp.bfloat16

D_MODEL = 1024
HEAD_DIM = 64
N_HEADS_A = 8
N_KV_A = 2
N_HEADS_B = 8
N_KV_B = 2
N_HEADS = N_HEADS_A + N_HEADS_B
WINDOW_A = 128
CMP_LEN = 32
CMP_STRIDE = 16
CMP_HIDDEN = 128
SEL_LEN = 64
SEL_TOPK = 16
WINDOW_B = 512
FORCE_BONUS = 1000.0
NUM_BUCKETS = 32
MAX_DISTANCE = 128
D_FF = 2816
EPS = 1e-6
NEG = -1e9
LOG2E = math.log2(math.e)

QA = N_HEADS_A * HEAD_DIM
KVA = N_KV_A * HEAD_DIM
QB = N_HEADS_B * HEAD_DIM
KVB = N_KV_B * HEAD_DIM
N_GATES = 3 * N_HEADS_B
OFF_KA = QA
OFF_VA = OFF_KA + KVA
OFF_QB = OFF_VA + KVA
OFF_KVB = OFF_QB + QB
OFF_GB = OFF_KVB + 6 * KVB
D_IN = OFF_GB + N_GATES

LANES = 128
TQ = 128
FAR_TK = 512
NSA_TILES = 2
SWA_TILES = 4
FFN_TM = 512
FFN_TF = 256
VMEM_LIMIT = 56 * 1024 * 1024

P_QA = 0
P_QB = P_QA + QA
P_KA = P_QB + QB
P_KS = P_KA + LANES
P_KW = P_KS + LANES
P_VA = P_KW + LANES
P_VS = P_VA + LANES
P_VW = P_VS + LANES
P_KVC = P_VW + LANES
P_GATE = P_KVC + 2 * LANES
P_TOTAL = P_GATE + LANES
P_NORM = P_VA


def _nt_dot(a, b):
    return lax.dot_general(a, b, (((1,), (1,)), ((), ())), preferred_element_type=F32)


def _dot(a, b):
    return jnp.dot(a, b, preferred_element_type=F32)


def _rms_rows(x, g):
    ms = jnp.mean(x * x, axis=-1, keepdims=True)
    return x * lax.rsqrt(ms + EPS) * g


def _bucket_np(rel):
    n = np.maximum(rel, 0)
    max_exact = NUM_BUCKETS // 2
    nf = np.maximum(n, 1).astype(np.float64)
    large = max_exact + (np.log(nf / max_exact) / math.log(MAX_DISTANCE / max_exact)
                         * (NUM_BUCKETS - max_exact)).astype(np.int64)
    large = np.minimum(large, NUM_BUCKETS - 1)
    return np.where(n < max_exact, n, large).astype(np.int32)


def _ffn_body(*refs, pre):
    if pre:
        x_ref, ma_ref, mb_ref, wo_ref, nrm_ref, win_ref, wout_ref, o_ref, acc_ref = refs
    else:
        x_ref, nrm_ref, win_ref, wout_ref, o_ref, acc_ref = refs
    x = x_ref[...]
    if pre:
        x = x + _dot(ma_ref[...], wo_ref[0:QA, :]) + _dot(mb_ref[...], wo_ref[QA:QA + QB, :])
    h = _rms_rows(x, nrm_ref[...]).astype(BF16)
    for c in range(D_FF // FFN_TF):
        gate = _dot(h, win_ref[:, c * FFN_TF:(c + 1) * FFN_TF])
        up = _dot(h, win_ref[:, D_FF + c * FFN_TF:D_FF + (c + 1) * FFN_TF])
        a = (gate * jax.nn.sigmoid(gate) * up).astype(BF16)
        part = _dot(a, wout_ref[c * FFN_TF:(c + 1) * FFN_TF, :])
        if c == 0:
            acc_ref[...] = part
        else:
            acc_ref[...] += part
    o_ref[...] = x + 0.5 * acc_ref[...]


def _ffn(x2d, nrm, w_in, w_out, pre=None):
    m = x2d.shape[0]
    tm = FFN_TM
    assert m % tm == 0
    row = lambda i: (i, 0)
    const = lambda i: (0, 0)
    in_specs = [pl.BlockSpec((tm, D_MODEL), row)]
    args = [x2d]
    if pre is not None:
        mix_a, mix_b, wo = pre
        in_specs += [pl.BlockSpec((tm, QA), row), pl.BlockSpec((tm, QB), row),
                     pl.BlockSpec((QA + QB, D_MODEL), const)]
        args += [mix_a, mix_b, wo]
    in_specs += [pl.BlockSpec((1, D_MODEL), const),
                 pl.BlockSpec((D_MODEL, 2 * D_FF), const),
                 pl.BlockSpec((D_FF, D_MODEL), const)]
    args += [nrm.reshape(1, D_MODEL), w_in, w_out]
    return pl.pallas_call(
        functools.partial(_ffn_body, pre=pre is not None),
        out_shape=jax.ShapeDtypeStruct((m, D_MODEL), F32),
        grid=(m // tm,),
        in_specs=in_specs,
        out_specs=pl.BlockSpec((tm, D_MODEL), row),
        scratch_shapes=[pltpu.VMEM((tm, D_MODEL), F32)],
        compiler_params=pltpu.CompilerParams(
            dimension_semantics=("arbitrary",), vmem_limit_bytes=VMEM_LIMIT),
        name="ffn_pre" if pre is not None else "ffn",
    )(*args)


def _proj_body(x_ref, nrm_ref, w_ref, gain_ref, qa_ref, qb_ref, ka_ref, ks_ref, kw_ref,
               va_ref, vs_ref, vw_ref, kvc_ref, gate_ref):
    tm = x_ref.shape[0]
    h = _rms_rows(x_ref[...], nrm_ref[...]).astype(BF16)
    lo = lax.broadcasted_iota(jnp.int32, (tm, LANES), 1) < HEAD_DIM
    inv_d = 1.0 / HEAD_DIM

    def pair(c0):
        p2 = _dot(h, w_ref[:, c0:c0 + 2 * LANES])
        return p2[:, :LANES], p2[:, LANES:]

    def two_heads_f32(p, c0):
        sq = p * p
        ms0 = jnp.sum(jnp.where(lo, sq, 0.0), axis=-1, keepdims=True) * inv_d
        ms1 = jnp.sum(jnp.where(lo, 0.0, sq), axis=-1, keepdims=True) * inv_d
        r = jnp.where(lo, lax.rsqrt(ms0 + EPS), lax.rsqrt(ms1 + EPS))
        return p * r * gain_ref[:, c0:c0 + LANES]

    def two_heads(p, c0):
        return two_heads_f32(p, c0).astype(BF16)

    def spread_query_heads(p, c0, ref, t, n_kv):
        y = two_heads_f32(p, c0)
        swapped = pltpu.roll(y, HEAD_DIM, 1)
        g = (2 * t) // (ref.shape[1] // LANES // n_kv)
        even = jnp.where(lo, y, 0.0) if g == 0 else jnp.where(lo, 0.0, swapped)
        odd = jnp.where(lo, swapped, 0.0) if g == 0 else jnp.where(lo, 0.0, y)
        ref[:, 2 * t * LANES:(2 * t + 1) * LANES] = even.astype(BF16)
        ref[:, (2 * t + 1) * LANES:(2 * t + 2) * LANES] = odd.astype(BF16)

    for ref, base, n_kv in ((qa_ref, P_QA, N_KV_A), (qb_ref, P_QB, N_KV_B)):
        for pc in range(ref.shape[1] // LANES // 4):
            c0 = base + pc * 2 * LANES
            pa, pb = pair(c0)
            spread_query_heads(pa, c0, ref, 2 * pc, n_kv)
            spread_query_heads(pb, c0 + LANES, ref, 2 * pc + 1, n_kv)
    pa, pb = pair(P_KA)
    ka_ref[...] = two_heads(pa, P_KA)
    ks_ref[...] = two_heads(pb, P_KS)
    pa, pb = pair(P_KW)
    kw_ref[...] = two_heads(pa, P_KW)
    va_ref[...] = pb.astype(BF16)
    pa, pb = pair(P_VS)
    vs_ref[...] = pa.astype(BF16)
    vw_ref[...] = pb.astype(BF16)
    pa, pb = pair(P_KVC)
    kvc_ref[:, :LANES] = pa
    kvc_ref[:, LANES:] = pb
    pg = _dot(h, w_ref[:, P_GATE:P_GATE + LANES])
    gate_ref[...] = jax.nn.sigmoid(pg)


def _proj_src_columns():
    src = np.full((P_TOTAL,), -1, np.int64)
    gidx = np.full((P_NORM,), -1, np.int64)
    d = np.arange(HEAD_DIM)
    for hh in range(N_HEADS_A):
        src[P_QA + hh * HEAD_DIM + d] = hh * HEAD_DIM + d
        gidx[P_QA + hh * HEAD_DIM + d] = d
    for hh in range(N_HEADS_B):
        src[P_QB + hh * HEAD_DIM + d] = OFF_QB + hh * HEAD_DIM + d
        gidx[P_QB + hh * HEAD_DIM + d] = HEAD_DIM + d
    two = np.arange(2 * HEAD_DIM)
    src[P_KA + two] = OFF_KA + two
    gidx[P_KA + two] = 2 * HEAD_DIM + two % HEAD_DIM
    src[P_VA + two] = OFF_VA + two
    for slot, dst in ((0, P_KVC), (1, P_KVC + LANES), (2, P_KS), (3, P_VS), (4, P_KW), (5, P_VW)):
        src[dst + two] = OFF_KVB + slot * KVB + two
    gidx[P_KS + two] = 3 * HEAD_DIM + two % HEAD_DIM
    gidx[P_KW + two] = 3 * HEAD_DIM + two % HEAD_DIM
    src[P_GATE + np.arange(N_GATES)] = OFF_GB + np.arange(N_GATES)
    return src, gidx


def _proj(x2d, nrm, w_mix_in, q_norm_a, k_norm_a, q_norm_b, k_norm_b):
    m = x2d.shape[0]
    tm = FFN_TM
    src, gidx = _proj_src_columns()
    pieces, c = [], 0
    while c < P_TOTAL:
        e = c + 1
        while e < P_TOTAL and ((src[e] < 0) == (src[c] < 0)) and (src[c] < 0 or src[e] == src[e - 1] + 1):
            e += 1
        pieces.append(jnp.zeros((D_MODEL, e - c), w_mix_in.dtype) if src[c] < 0
                      else w_mix_in[:, src[c]:src[c] + e - c])
        c = e
    w = jnp.concatenate(pieces, axis=1).astype(BF16)
    scale = HEAD_DIM ** -0.5
    gsrc = jnp.concatenate([q_norm_a * (scale * LOG2E), q_norm_b * (scale * LOG2E), k_norm_a, k_norm_b]).astype(F32)
    gain = jnp.where(jnp.asarray(gidx >= 0), gsrc[np.maximum(gidx, 0)], 0.0).reshape(1, P_NORM)
    row = lambda i: (i, 0)
    const = lambda i: (0, 0)
    widths = [(N_HEADS_A * LANES, BF16), (N_HEADS_B * LANES, BF16)] + [(LANES, BF16)] * 6 + \
             [(2 * LANES, F32), (LANES, F32)]
    return pl.pallas_call(
        _proj_body,
        out_shape=[jax.ShapeDtypeStruct((m, wd), dt) for wd, dt in widths],
        grid=(m // tm,),
        in_specs=[pl.BlockSpec((tm, D_MODEL), row), pl.BlockSpec((1, D_MODEL), const),
                  pl.BlockSpec((D_MODEL, P_TOTAL), const), pl.BlockSpec((1, P_NORM), const)],
        out_specs=[pl.BlockSpec((tm, wd), row) for wd, _ in widths],
        compiler_params=pltpu.CompilerParams(
            dimension_semantics=("arbitrary",), vmem_limit_bytes=VMEM_LIMIT),
        name="proj",
    )(x2d, nrm.reshape(1, D_MODEL), w, gain)


CMP_FEAT = CMP_STRIDE * 2 * LANES
CMP_SLOTS = 4


def _compress_body(x_ref, pos_ref, w1_ref, b1_ref, w2_ref, b2_ref, gain_ref, kc_ref, vc_ref):
    nck = x_ref.shape[1]
    x = x_ref[0]
    a = _dot((x + pos_ref[0]).astype(BF16), w1_ref[0])
    b = _dot((x + pos_ref[1]).astype(BF16), w1_ref[1])
    pre = a + pltpu.roll(b, nck - 1, 0) + b1_ref[...]
    hid = (pre * jax.nn.sigmoid(pre)).astype(BF16)
    out = _dot(hid, w2_ref[...]) + b2_ref[...]
    k = out[:, :LANES]
    lo = lax.broadcasted_iota(jnp.int32, (nck, LANES), 1) < HEAD_DIM
    sq = k * k
    ms0 = jnp.sum(jnp.where(lo, sq, 0.0), axis=-1, keepdims=True) * (1.0 / HEAD_DIM)
    ms1 = jnp.sum(jnp.where(lo, 0.0, sq), axis=-1, keepdims=True) * (1.0 / HEAD_DIM)
    r = jnp.where(lo, lax.rsqrt(ms0 + EPS), lax.rsqrt(ms1 + EPS))
    kc_ref[0] = (k * r * gain_ref[...]).astype(BF16)
    vc_ref[0] = out[:, LANES:].astype(BF16)


def _compress(kvc, cmp_pos, cmp_w1, cmp_b1, cmp_w2, cmp_b2, k_norm_b):
    b, s, _ = kvc.shape
    nck = s // CMP_STRIDE
    x = kvc.reshape(b, nck, CMP_FEAT)
    r = CMP_LEN // CMP_STRIDE
    w1r = cmp_w1.reshape(2, r, CMP_STRIDE, HEAD_DIM, CMP_HIDDEN)
    eye = jnp.eye(CMP_SLOTS, dtype=F32)
    kv_of_slot = np.array([0, 0, 1, 1])
    w1s = w1r[kv_of_slot]
    w1big = jnp.einsum('sjpdh,st->jpsdth', w1s, eye).reshape(r, CMP_FEAT, CMP_SLOTS * CMP_HIDDEN)
    posr = cmp_pos.reshape(2, r, CMP_STRIDE, HEAD_DIM)[kv_of_slot]
    posbig = posr.transpose(1, 2, 0, 3).reshape(r, 1, CMP_FEAT)
    b1big = cmp_b1[kv_of_slot].reshape(1, CMP_SLOTS * CMP_HIDDEN)
    w2big = jnp.einsum('shd,st->shtd', cmp_w2[kv_of_slot], eye).reshape(
        CMP_SLOTS * CMP_HIDDEN, CMP_SLOTS * HEAD_DIM)
    b2big = cmp_b2[kv_of_slot].reshape(1, CMP_SLOTS * HEAD_DIM)
    gain = jnp.tile(k_norm_b, 2).reshape(1, LANES)
    assert r == 2
    c3 = lambda i: (0, 0, 0)
    c2 = lambda i: (0, 0)
    return pl.pallas_call(
        _compress_body,
        out_shape=[jax.ShapeDtypeStruct((b, nck, LANES), BF16)] * 2,
        grid=(b,),
        in_specs=[pl.BlockSpec((1, nck, CMP_FEAT), lambda i: (i, 0, 0)),
                  pl.BlockSpec((r, 1, CMP_FEAT), c3),
                  pl.BlockSpec((r, CMP_FEAT, CMP_SLOTS * CMP_HIDDEN), c3),
                  pl.BlockSpec((1, CMP_SLOTS * CMP_HIDDEN), c2),
                  pl.BlockSpec((CMP_SLOTS * CMP_HIDDEN, CMP_SLOTS * HEAD_DIM), c2),
                  pl.BlockSpec((1, CMP_SLOTS * HEAD_DIM), c2),
                  pl.BlockSpec((1, LANES), c2)],
        out_specs=[pl.BlockSpec((1, nck, LANES), lambda i: (i, 0, 0))] * 2,
        compiler_params=pltpu.CompilerParams(
            dimension_semantics=("arbitrary",), vmem_limit_bytes=VMEM_LIMIT),
        name="compress",
    )(x, posbig, w1big.astype(BF16), b1big, w2big.astype(BF16), b2big, gain)


def _merge_heads(o_heads, g, lane):
    tiles = []
    for c in range(len(o_heads) // 2):
        lo_src = o_heads[2 * c] if g == 0 else pltpu.roll(o_heads[2 * c], HEAD_DIM, 1)
        hi_src = o_heads[2 * c + 1] if g == 1 else pltpu.roll(o_heads[2 * c + 1], HEAD_DIM, 1)
        tiles.append(jnp.where(lane < HEAD_DIM, lo_src, hi_src))
    return tiles


def _band_tiles(tbl):
    i = np.arange(TQ)[:, None]
    j = np.arange(TQ)[None, :]
    return _table_lookup(tbl, _bucket_np(i - j + TQ)), _table_lookup(tbl, _bucket_np(i - j))


def _table_lookup(tbl, buckets):
    onehot = (buckets.reshape(-1)[:, None] == np.arange(NUM_BUCKETS)[None, :]).astype(np.float32)
    out = jnp.einsum('pk,kh->hp', jnp.asarray(onehot), tbl.astype(F32), precision=lax.Precision.HIGHEST)
    return out.reshape((tbl.shape[1],) + buckets.shape)


def _swa_body(sink_ref, q_ref, k_ref, v_ref, bias_ref, o_ref):
    ii = lax.broadcasted_iota(jnp.int32, (TQ, 2 * TQ), 0)
    jj = lax.broadcasted_iota(jnp.int32, (TQ, 2 * TQ), 1)
    lane = lax.broadcasted_iota(jnp.int32, (TQ, LANES), 1)
    rpg = N_HEADS_A // N_KV_A
    for sub in range(SWA_TILES):
        qt = pl.program_id(1) * SWA_TILES + sub
        t0 = qt * TQ
        rows = slice(sub * TQ, (sub + 1) * TQ)
        s_prev = pl.multiple_of(jnp.maximum(t0 - TQ, 0), TQ)
        s_diag = pl.multiple_of(t0, TQ)
        k2 = jnp.concatenate([k_ref[0, pl.ds(s_prev, TQ), :], k_ref[0, pl.ds(s_diag, TQ), :]], axis=0)
        v2 = jnp.concatenate([v_ref[0, pl.ds(s_prev, TQ), :], v_ref[0, pl.ds(s_diag, TQ), :]], axis=0)
        mask = ((jj < TQ) & (jj > ii) & (qt > 0)) | ((jj >= TQ) & ((jj - TQ) <= ii))
        q_all = jnp.concatenate(
            [q_ref[0, rows, hh * LANES:(hh + 1) * LANES] for hh in range(N_HEADS_A)], axis=0)
        s_all = _nt_dot(q_all, k2)
        ps, dens = [], []
        for hh in range(N_HEADS_A):
            s = jnp.where(mask, s_all[hh * TQ:(hh + 1) * TQ] + bias_ref[hh], NEG)
            sink = sink_ref[hh]
            m = jnp.maximum(jnp.max(s, axis=-1, keepdims=True), sink)
            p = jnp.exp2(s - m)
            dens.append(jnp.sum(p, axis=-1, keepdims=True) + jnp.exp2(sink - m))
            ps.append(p.astype(BF16))
        o_all = _dot(jnp.concatenate(ps, axis=0), v2)
        for g in range(N_KV_A):
            o_heads = [o_all[(g * rpg + r) * TQ:(g * rpg + r + 1) * TQ] / dens[g * rpg + r]
                       for r in range(rpg)]
            for c, tile in enumerate(_merge_heads(o_heads, g, lane)):
                col = (g * rpg // 2 + c) * LANES
                o_ref[0, rows, col:col + LANES] = tile.astype(BF16)


def _swa(qa, ka, va, bias, sinks):
    b, s, _ = qa.shape
    assert WINDOW_A == TQ
    return pl.pallas_call(
        _swa_body,
        out_shape=jax.ShapeDtypeStruct((b, s, QA), BF16),
        grid=(b, s // (SWA_TILES * TQ)),
        in_specs=[pl.BlockSpec(memory_space=pltpu.SMEM),
                  pl.BlockSpec((1, SWA_TILES * TQ, N_HEADS_A * LANES), lambda i, j: (i, j, 0)),
                  pl.BlockSpec((1, s, LANES), lambda i, j: (i, 0, 0)),
                  pl.BlockSpec((1, s, LANES), lambda i, j: (i, 0, 0)),
                  pl.BlockSpec((N_HEADS_A, TQ, 2 * TQ), lambda i, j: (0, 0, 0))],
        out_specs=pl.BlockSpec((1, SWA_TILES * TQ, QA), lambda i, j: (i, j, 0)),
        compiler_params=pltpu.CompilerParams(
            dimension_semantics=("arbitrary", "arbitrary"), vmem_limit_bytes=VMEM_LIMIT),
        name="swa",
    )(sinks.astype(F32), qa, ka, va, bias)


CMP_NEAR = 16
CMP_NEAR_LO = 9


def _nsa_select(sub, q_ref, kc_ref, vc_ref, ks_ref, vs_ref, kw_ref, vw_ref, oh_ref,
                bprev_ref, bdiag_ref, g2_ref, ovl_ref, m_ref, acc_ref, w_ref, *, n_slc, n_top):
    qt = pl.program_id(1) * NSA_TILES + sub
    R = N_HEADS_B * TQ
    t0 = qt * TQ
    rpg = N_HEADS_B // N_KV_B
    ncmp = kc_ref.shape[1]
    nwb = WINDOW_B // TQ
    s_prev = pl.multiple_of(jnp.maximum(t0 - TQ, 0), TQ)
    s_diag = pl.multiple_of(t0, TQ)
    lane = lax.broadcasted_iota(jnp.int32, (TQ, LANES), 1)
    rowi = lax.broadcasted_iota(jnp.int32, (TQ, LANES), 0)
    causal = lane <= rowi
    ones_col = (lane == 0).astype(BF16)
    has_prev = qt > 0

    c_idx = lax.broadcasted_iota(jnp.int32, (ncmp, LANES), 0)
    l_idx = lax.broadcasted_iota(jnp.int32, (ncmp, LANES), 1)
    place = ((c_idx - (TQ // CMP_STRIDE) * qt + CMP_NEAR_LO) == (l_idx & (CMP_NEAR - 1))) \
        & (l_idx < 2 * CMP_NEAR)
    kc_aug = jnp.concatenate([kc_ref[0], place.astype(BF16)], axis=1)
    qi = lax.broadcasted_iota(jnp.int32, (TQ, ncmp), 0)
    ci = lax.broadcasted_iota(jnp.int32, (TQ, ncmp), 1)
    mask_c = (t0 + qi - CMP_STRIDE * ci - (CMP_LEN - 1)) >= 0
    vc = vc_ref[0]

    cur = (t0 + rowi) // SEL_LEN
    valid = (lane <= cur) & (lane < n_slc)
    forced = (lane == 0) | (lane == cur) | (lane == cur - 1)
    cur0 = t0 // SEL_LEN
    near_lo = cur0 - TQ // SEL_LEN

    ks2 = jnp.concatenate(
        [jnp.concatenate([ks_ref[0, pl.ds(s_prev, TQ), :], oh_ref[pl.ds(s_prev, TQ), :]], axis=1),
         jnp.concatenate([ks_ref[0, pl.ds(s_diag, TQ), :], oh_ref[pl.ds(s_diag, TQ), :]], axis=1)],
        axis=0)
    vs2 = jnp.concatenate(
        [jnp.concatenate([vs_ref[0, pl.ds(s_prev, TQ), :], ones_col], axis=1),
         jnp.concatenate([vs_ref[0, pl.ds(s_diag, TQ), :], ones_col], axis=1)], axis=0)
    ones_far = jnp.concatenate([ones_col] * (FAR_TK // TQ), axis=0)

    w_starts = [pl.multiple_of(jnp.maximum(t0 - kb * TQ, 0), TQ) for kb in range(nwb, -1, -1)]
    kw5 = jnp.concatenate([kw_ref[0, pl.ds(st, TQ), :] for st in w_starts], axis=0)
    vw5 = jnp.concatenate([vw_ref[0, pl.ds(st, TQ), :] for st in w_starts], axis=0)

    nh = N_HEADS_B
    q_all = jnp.concatenate(
        [q_ref[0, sub * TQ:(sub + 1) * TQ, hh * LANES:(hh + 1) * LANES] for hh in range(nh)], axis=0)
    g2_all = jnp.concatenate([g2_ref[g] for g in range(N_KV_B)], axis=0)

    lc_all = _nt_dot(jnp.concatenate([q_all, g2_all], axis=1), kc_aug)
    w_ref[sub * R:(sub + 1) * R, :] = _nt_dot(q_all, kw5)
    pcs = [None] * N_KV_B
    pc_rows = []
    for hh in range(nh):
        lc = jnp.where(mask_c, lc_all[hh * TQ:(hh + 1) * TQ], NEG)
        e = jnp.where(mask_c, jnp.exp2(lc - jnp.max(lc, axis=-1, keepdims=True)), 0.0)
        den = jnp.maximum(jnp.sum(e, axis=-1, keepdims=True), jnp.finfo(F32).tiny)
        pc = e / den
        pcs[hh // rpg] = pc if pcs[hh // rpg] is None else pcs[hh // rpg] + pc
        pc_rows.append(pc.astype(BF16))
    o_cmp_all = _dot(jnp.concatenate(pc_rows, axis=0), vc)

    pcs_all = jnp.concatenate(pcs, axis=0)
    pcs_hi = pcs_all.astype(BF16)
    pcs_lo = (pcs_all - pcs_hi.astype(F32)).astype(BF16)
    imp_all = _dot(pcs_hi, ovl_ref[...]) + _dot(pcs_lo, ovl_ref[...])
    s_ts = []
    for g in range(N_KV_B):
        score = jnp.where(valid, imp_all[g * TQ:(g + 1) * TQ] + jnp.where(forced, FORCE_BONUS, 0.0), NEG)
        s_ts.append(score.T[:n_slc])
    s_t = jnp.concatenate(s_ts, axis=1)
    n_io = lax.broadcasted_iota(jnp.int32, (n_slc, N_KV_B * TQ), 0)
    n_acc = 4
    ranks = [jnp.zeros((n_slc, N_KV_B * TQ), F32) for _ in range(n_acc)]
    for mm in range(n_slc):
        sm = s_t[mm:mm + 1, :]
        beats = (sm > s_t) | ((sm == s_t) & (n_io > mm))
        ranks[mm % n_acc] = ranks[mm % n_acc] + jnp.where(beats, 1.0, 0.0)
    rank = (ranks[0] + ranks[1]) + (ranks[2] + ranks[3])
    sel_t = jnp.where((rank < n_top) & (s_t > 0.5 * NEG), 1.0, 0.0)
    m_nears, m_fars = [], []
    for g in range(N_KV_B):
        st = sel_t[:, g * TQ:(g + 1) * TQ]
        if n_slc < LANES:
            st = jnp.concatenate([st, jnp.zeros((LANES - n_slc, TQ), F32)], axis=0)
        sel = st.T > 0.5
        m_nears.append(jnp.where(sel & (lane >= near_lo), 0.0, NEG).astype(BF16))
        m_fars.append(jnp.where(sel & (lane < near_lo), 0.0, NEG).astype(BF16))
    q_near = jnp.concatenate(
        [q_all, jnp.concatenate([m_nears[hh // rpg] for hh in range(nh)], axis=0)], axis=1)
    q_far = jnp.concatenate(
        [q_all, jnp.concatenate([m_fars[hh // rpg] for hh in range(nh)], axis=0)], axis=1)

    s_all = _nt_dot(q_near, ks2)
    s_rows = []
    for hh in range(nh):
        sr = s_all[hh * TQ:(hh + 1) * TQ]
        s_p = jnp.where(has_prev, sr[:, :TQ] + bprev_ref[hh], NEG)
        s_d = jnp.where(causal, sr[:, TQ:] + bdiag_ref[hh], NEG)
        s_rows.append(jnp.concatenate([s_p, s_d], axis=1))
    s_near = jnp.concatenate(s_rows, axis=0)
    m0 = jnp.max(s_near, axis=-1, keepdims=True)
    m_ref[sub * R:(sub + 1) * R, :] = jnp.broadcast_to(m0, (nh * TQ, LANES))
    acc_ref[sub * R:(sub + 1) * R, :] = _dot(jnp.exp2(s_near - m0).astype(BF16), vs2)
    return q_far, (qt, o_cmp_all, vw5)


def _nsa_body(q_ref, kc_ref, vc_ref, ks_ref, vs_ref, kw_ref, vw_ref, oh_ref, gate_ref,
              bprev_ref, bdiag_ref, g2_ref, ovl_ref, o_ref, m_ref, acc_ref,
              s0_ref, s1_ref, mc0_ref, mc1_ref, w_ref, *, n_slc, n_top):
    q_fars, saved = [], []
    for sub in range(NSA_TILES):
        q_far_t, keep = _nsa_select(sub, q_ref, kc_ref, vc_ref, ks_ref, vs_ref, kw_ref, vw_ref, oh_ref,
                                    bprev_ref, bdiag_ref, g2_ref, ovl_ref, m_ref, acc_ref, w_ref,
                                    n_slc=n_slc, n_top=n_top)
        q_fars.append(q_far_t)
        saved.append(keep)
    q_far = jnp.concatenate(q_fars, axis=0)
    qt = saved[-1][0]
    lane = lax.broadcasted_iota(jnp.int32, (TQ, LANES), 1)
    ones_far = jnp.concatenate([(lane == 0).astype(BF16)] * (FAR_TK // TQ), axis=0)

    n_far = (qt + FAR_TK // TQ - 2) // (FAR_TK // TQ)
    last_chunk = ks_ref.shape[1] // FAR_TK - 1
    s_bufs = (s0_ref, s1_ref)
    mc_bufs = (mc0_ref, mc1_ref)

    def chunk_start(c):
        return pl.multiple_of(jnp.minimum(c, last_chunk) * FAR_TK, FAR_TK)

    def produce(slot, c):
        k0 = chunk_start(c)
        kk = jnp.concatenate([ks_ref[0, pl.ds(k0, FAR_TK), :], oh_ref[pl.ds(k0, FAR_TK), :]], axis=1)
        s = _nt_dot(q_far, kk)
        s_bufs[slot][...] = s
        mc_bufs[slot][...] = jnp.broadcast_to(jnp.max(s, axis=-1, keepdims=True), (s.shape[0], LANES))

    def consume(slot, c):
        k0 = chunk_start(c)
        vv = jnp.concatenate([vs_ref[0, pl.ds(k0, FAR_TK), :], ones_far], axis=1)
        m_old = m_ref[...]
        m_new = jnp.maximum(m_old, mc_bufs[slot][...])
        alpha = jnp.exp2(m_old - m_new)
        p = jnp.exp2(s_bufs[slot][...] - jnp.concatenate([m_new] * (FAR_TK // LANES), axis=1)).astype(BF16)
        acc_ref[...] = acc_ref[...] * jnp.concatenate([alpha] * 2, axis=1) + _dot(p, vv)
        m_ref[...] = m_new

    produce(0, 0)

    def far_pair(j, carry):
        produce(1, 2 * j + 1)
        consume(0, 2 * j)
        produce(0, 2 * j + 2)
        consume(1, 2 * j + 1)
        return carry

    lax.fori_loop(0, n_far // 2, far_pair, 0)

    @pl.when(n_far % 2 == 1)
    def _():
        consume(0, n_far - 1)

    acc = acc_ref[...]
    o_sel_all = acc[:, :LANES] / acc[:, LANES:LANES + 1]
    for sub in range(NSA_TILES):
        _nsa_finish(sub, saved[sub], o_sel_all, vw_ref, gate_ref, bprev_ref, bdiag_ref, w_ref, o_ref)


def _nsa_finish(sub, kept, o_sel_all, vw_ref, gate_ref, bprev_ref, bdiag_ref, w_ref, o_ref):
    qt, o_cmp_all, vw5 = kept
    nh = N_HEADS_B
    rpg = N_HEADS_B // N_KV_B
    nwb = WINDOW_B // TQ
    R = nh * TQ
    lane = lax.broadcasted_iota(jnp.int32, (TQ, LANES), 1)
    rowi = lax.broadcasted_iota(jnp.int32, (TQ, LANES), 0)
    causal = lane <= rowi
    has_prev = qt > 0

    pw, lw = [], []
    for hh in range(nh):
        wr = w_ref[sub * R + hh * TQ:sub * R + (hh + 1) * TQ, :]
        blocks = []
        for bi, kb in enumerate(range(nwb, -1, -1)):
            blk = wr[:, bi * TQ:(bi + 1) * TQ]
            if kb == nwb:
                blk = jnp.where((lane > rowi) & (qt >= kb), blk, NEG)
            elif kb >= 2:
                blk = jnp.where(qt >= kb, blk, NEG)
            elif kb == 1:
                blk = jnp.where(has_prev, blk + bprev_ref[hh], NEG)
            else:
                blk = jnp.where(causal, blk + bdiag_ref[hh], NEG)
            blocks.append(blk)
        sw = jnp.concatenate(blocks, axis=1)
        p = jnp.exp2(sw - jnp.max(sw, axis=-1, keepdims=True))
        lw.append(jnp.sum(p, axis=-1, keepdims=True))
        pw.append(p.astype(BF16))
    o_win_all = _dot(jnp.concatenate(pw, axis=0), vw5)

    for g in range(N_KV_B):
        o_heads = []
        for r in range(rpg):
            hh = g * rpg + r
            sl = slice(hh * TQ, (hh + 1) * TQ)
            qrows = slice(sub * TQ, (sub + 1) * TQ)
            g_cmp = gate_ref[0, qrows, hh:hh + 1]
            g_sel = gate_ref[0, qrows, N_HEADS_B + hh:N_HEADS_B + hh + 1]
            g_win = gate_ref[0, qrows, 2 * N_HEADS_B + hh:2 * N_HEADS_B + hh + 1]
            o_sel = o_sel_all[sub * R + hh * TQ:sub * R + (hh + 1) * TQ]
            o_heads.append(g_cmp * o_cmp_all[sl] + g_sel * o_sel + g_win * (o_win_all[sl] / lw[hh]))
        for c, tile in enumerate(_merge_heads(o_heads, g, lane)):
            col = (g * rpg // 2 + c) * LANES
            o_ref[0, sub * TQ:(sub + 1) * TQ, col:col + LANES] = tile.astype(BF16)


def _nsa(qb, kc, vc, ks, vs, kw, vw, gates, bprev, bdiag, g2):
    b, s, _ = qb.shape
    assert s % FAR_TK == 0 and WINDOW_B % TQ == 0 and TQ % SEL_LEN == 0
    assert NSA_TILES == 2 and (FAR_TK // TQ) % NSA_TILES == 0
    n_slc = s // SEL_LEN
    assert n_slc <= LANES and n_slc % 8 == 0
    n_top = min(SEL_TOPK, n_slc)
    ncmp = s // CMP_STRIDE
    rpg = N_HEADS_B // N_KV_B
    kpos = np.arange(s)
    onehot = (kpos[:, None] // SEL_LEN == np.arange(LANES)[None, :]).astype(np.float32)
    cstart = np.arange(ncmp) * CMP_STRIDE
    sstart = np.arange(LANES) * SEL_LEN
    ovl = ((cstart[:, None] < sstart[None, :] + SEL_LEN)
           & (cstart[:, None] + CMP_LEN - 1 >= sstart[None, :])
           & (np.arange(ncmp)[:, None] < ncmp - CMP_LEN // CMP_STRIDE + 1)
           & (np.arange(LANES)[None, :] < n_slc)).astype(np.float32)
    full = lambda i, j: (i, 0, 0)
    return pl.pallas_call(
        functools.partial(_nsa_body, n_slc=n_slc, n_top=n_top),
        out_shape=jax.ShapeDtypeStruct((b, s, QB), BF16),
        grid=(b, s // (NSA_TILES * TQ)),
        in_specs=[pl.BlockSpec((1, NSA_TILES * TQ, N_HEADS_B * LANES), lambda i, j: (i, j, 0)),
                  pl.BlockSpec((1, ncmp, LANES), full), pl.BlockSpec((1, ncmp, LANES), full),
                  pl.BlockSpec((1, s, LANES), full), pl.BlockSpec((1, s, LANES), full),
                  pl.BlockSpec((1, s, LANES), full), pl.BlockSpec((1, s, LANES), full),
                  pl.BlockSpec((s, LANES), lambda i, j: (0, 0)),
                  pl.BlockSpec((1, NSA_TILES * TQ, LANES), lambda i, j: (i, j, 0)),
                  pl.BlockSpec((N_HEADS_B, TQ, TQ), lambda i, j: (0, 0, 0)),
                  pl.BlockSpec((N_HEADS_B, TQ, TQ), lambda i, j: (0, 0, 0)),
                  pl.BlockSpec((N_KV_B, rpg * TQ, LANES), lambda i, j: (0, 0, 0)),
                  pl.BlockSpec((ncmp, LANES), lambda i, j: (0, 0))],
        out_specs=pl.BlockSpec((1, NSA_TILES * TQ, QB), lambda i, j: (i, j, 0)),
        scratch_shapes=[pltpu.VMEM((NSA_TILES * N_KV_B * rpg * TQ, LANES), F32),
                        pltpu.VMEM((NSA_TILES * N_KV_B * rpg * TQ, 2 * LANES), F32),
                        pltpu.VMEM((NSA_TILES * N_KV_B * rpg * TQ, FAR_TK), F32),
                        pltpu.VMEM((NSA_TILES * N_KV_B * rpg * TQ, FAR_TK), F32),
                        pltpu.VMEM((NSA_TILES * N_KV_B * rpg * TQ, LANES), F32),
                        pltpu.VMEM((NSA_TILES * N_KV_B * rpg * TQ, LANES), F32),
                        pltpu.VMEM((NSA_TILES * N_KV_B * rpg * TQ, WINDOW_B + TQ), F32)],
        compiler_params=pltpu.CompilerParams(
            dimension_semantics=("arbitrary", "arbitrary"), vmem_limit_bytes=VMEM_LIMIT),
        name="nsa",
    )(qb, kc, vc, ks, vs, kw, vw, jnp.asarray(onehot, BF16), gates, bprev, bdiag, g2,
      jnp.asarray(ovl, BF16))


def _cmp_bias_operand(tbl_b):
    i = np.arange(TQ)[:, None]
    u = np.arange(CMP_NEAR)[None, :]
    rel = i - CMP_STRIDE * (u - CMP_NEAR_LO) - (CMP_LEN - 1)
    assert rel[0, 0] + CMP_STRIDE >= MAX_DISTANCE and rel[-1, -1] - CMP_STRIDE < 0
    near = (rel >= 0) & (rel < MAX_DISTANCE)
    far_const = tbl_b[NUM_BUCKETS - 1]
    vals = jnp.where(jnp.asarray(near)[None], _table_lookup(tbl_b, _bucket_np(rel))
                     - far_const[:, None, None], 0.0)
    vals = vals * LOG2E
    hi = vals.astype(BF16)
    lo = (vals - hi.astype(F32)).astype(BF16)
    pad = jnp.zeros((N_HEADS_B, TQ, LANES - 2 * CMP_NEAR), BF16)
    out = jnp.concatenate([hi, lo, pad], axis=-1)
    rpg = N_HEADS_B // N_KV_B
    return out.reshape(N_KV_B, rpg * TQ, LANES)


def _layer(x, tbl, ffn1_norm, ffn1_w_in, ffn1_w_out, mix_norm, w_mix_in, w_mix_out,
           q_norm_a, k_norm_a, sinks_a, q_norm_b, k_norm_b,
           cmp_pos, cmp_w1, cmp_b1, cmp_w2, cmp_b2, ffn2_norm, ffn2_w_in, ffn2_w_out):
    b, s, d = x.shape
    x2d = x.reshape(b * s, d)
    x1 = _ffn(x2d, ffn1_norm, ffn1_w_in.astype(BF16), ffn1_w_out.astype(BF16))
    qa, qb, ka, ks, kw, va, vs, vw, kvc, gates = _proj(
        x1, mix_norm, w_mix_in, q_norm_a, k_norm_a, q_norm_b, k_norm_b)
    r3 = lambda a: a.reshape(b, s, a.shape[-1])
    kc, vc = _compress(r3(kvc), cmp_pos, cmp_w1, cmp_b1, cmp_w2, cmp_b2, k_norm_b)

    prev_t, diag_t = _band_tiles(tbl)
    bias_a = jnp.concatenate([prev_t[:N_HEADS_A], diag_t[:N_HEADS_A]], axis=-1) * LOG2E
    far_b = tbl[NUM_BUCKETS - 1, N_HEADS_A:].astype(F32)[:, None, None]
    bprev_b = (prev_t[N_HEADS_A:] - far_b) * LOG2E
    bdiag_b = (diag_t[N_HEADS_A:] - far_b) * LOG2E
    g2 = _cmp_bias_operand(tbl[:, N_HEADS_A:])

    mix_a = _swa(r3(qa), r3(ka), r3(va), bias_a, sinks_a * LOG2E)
    mix_b = _nsa(r3(qb), kc, vc, r3(ks), r3(vs), r3(kw), r3(vw), r3(gates), bprev_b, bdiag_b, g2)
    y = _ffn(x1, ffn2_norm, ffn2_w_in.astype(BF16), ffn2_w_out.astype(BF16),
             pre=(mix_a.reshape(b * s, QA), mix_b.reshape(b * s, QB), w_mix_out.astype(BF16)))
    return y.reshape(b, s, d)


def kernel(x, rel_bias_table, ffn1_norm, ffn1_w_in, ffn1_w_out, mix_norm, w_mix_in, w_mix_out,
           q_norm_a, k_norm_a, sinks_a, q_norm_b, k_norm_b,
           cmp_pos, cmp_w1, cmp_b1, cmp_w2, cmp_b2, ffn2_norm, ffn2_w_in, ffn2_w_out):
    for l in range(ffn1_norm.shape[0]):
        x = _layer(x, rel_bias_table, ffn1_norm[l], ffn1_w_in[l], ffn1_w_out[l],
                   mix_norm[l], w_mix_in[l], w_mix_out[l],
                   q_norm_a[l], k_norm_a[l], sinks_a[l], q_norm_b[l], k_norm_b[l],
                   cmp_pos[l], cmp_w1[l], cmp_b1[l], cmp_w2[l], cmp_b2[l],
                   ffn2_norm[l], ffn2_w_in[l], ffn2_w_out[l])
    return x
```
